```python
import math
import jax, jax.numpy as jnp
from jax import lax
import numpy as np

D_MODEL = 1024
BATCH = 8
SEQ = 4096
DEPTH = 4

HEAD_DIM = 64
SB_HEADS = 8
SWA_HEADS = 8
SWA_KV_HEADS = 2
WINDOW = 128
BLOCK = 128
D_FF = 4 * D_MODEL
ROPE_THETA = 10000.0
NORM_EPS = 1e-6
N_BRANCHES = 2

SB_WIDTH = SB_HEADS * HEAD_DIM
SWA_Q_WIDTH = SWA_HEADS * HEAD_DIM
SWA_KV_WIDTH = SWA_KV_HEADS * HEAD_DIM
IN_WIDTH = 3 * SB_WIDTH + SWA_Q_WIDTH + 2 * SWA_KV_WIDTH + N_BRANCHES * D_MODEL

kernel_name = "hybrid_stickbreak_swa_sink_gated_trunk"


def rms_norm(x, g):
    xf = x.astype(jnp.float32)
    y = xf * lax.rsqrt(jnp.mean(xf * xf, axis=-1, keepdims=True) + NORM_EPS)
    return (y * g.astype(jnp.float32)).astype(x.dtype)


def rope_tables(seq):
    inv_freq = 1.0 / (ROPE_THETA ** (jnp.arange(0, HEAD_DIM, 2, dtype=jnp.float32) / HEAD_DIM))
    ang = jnp.arange(seq, dtype=jnp.float32)[:, None] * inv_freq[None, :]
    return jnp.cos(ang), jnp.sin(ang)


def apply_rope(x, cos, sin):
    c = cos[None, :, None, :].astype(x.dtype)
    s = sin[None, :, None, :].astype(x.dtype)
    x1, x2 = jnp.split(x, 2, axis=-1)
    return jnp.concatenate([x1 * c - x2 * s, x2 * c + x1 * s], axis=-1)


def stick_breaking_attention(q, k, v):
    B, S, H, d = q.shape
    scale = d ** -0.5
    outs = []
    for blk in range(S // BLOCK):
        t0, t1 = blk * BLOCK, (blk + 1) * BLOCK
        qb = q[:, t0:t1]
        kb = k[:, :t1]
        vb = v[:, :t1]
        z = jnp.einsum('bqhd,bshd->bhqs', qb, kb).astype(jnp.float32) * scale
        t_idx = jnp.arange(t0, t1)[:, None]
        s_idx = jnp.arange(t1)[None, :]
        strict = s_idx < t_idx
        log_keep = jnp.where(strict, -jax.nn.softplus(z), 0.0)
        tail = lax.cumsum(log_keep, axis=3, reverse=True) - log_keep
        w = jnp.where(strict, jnp.exp(jax.nn.log_sigmoid(z) + tail), 0.0)
        outs.append(jnp.einsum('bhqs,bshd->bqhd', w.astype(v.dtype), vb))
    return jnp.concatenate(outs, axis=1)


def sliding_window_sink_attention(q, k, v, sinks):
    B, S, Hq, d = q.shape
    G = Hq // SWA_KV_HEADS
    nb = S // BLOCK
    scale = d ** -0.5
    qb = q.reshape(B, nb, BLOCK, SWA_KV_HEADS, G, d)
    pad = ((0, 0), (BLOCK, 0), (0, 0), (0, 0))
    kb = jnp.pad(k, pad).reshape(B, nb + 1, BLOCK, SWA_KV_HEADS, d)
    vb = jnp.pad(v, pad).reshape(B, nb + 1, BLOCK, SWA_KV_HEADS, d)
    k_band = jnp.concatenate([kb[:, :-1], kb[:, 1:]], axis=2)
    v_band = jnp.concatenate([vb[:, :-1], vb[:, 1:]], axis=2)
    scores = jnp.einsum('bnqkgd,bnskd->bnkgqs', qb, k_band).astype(jnp.float32) * scale
    i = jnp.arange(BLOCK)[:, None]
    j = jnp.arange(2 * BLOCK)[None, :]
    rel = j - BLOCK - i
    in_window = (rel <= 0) & (rel > -WINDOW)
    key_pos = jnp.arange(nb)[:, None, None] * BLOCK + j[None] - BLOCK
    valid = in_window[None] & (key_pos >= 0)
    scores = jnp.where(valid[None, :, None, None], scores, -jnp.inf)
    sink = jnp.broadcast_to(sinks.astype(jnp.float32).reshape(1, 1, SWA_KV_HEADS, G, 1, 1),
                            scores.shape[:-1] + (1,))
    probs = jax.nn.softmax(jnp.concatenate([scores, sink], axis=-1), axis=-1)[..., :-1]
    out = jnp.einsum('bnkgqs,bnskd->bnqkgd', probs.astype(v.dtype), v_band)
    return out.reshape(B, S, Hq * d)


def setup_inputs(seed: int = 0) -> dict:
    key = jax.random.key(seed)
    ks = jax.random.split(key, 12)
    nrm = lambda k, shape, scale: jax.random.normal(k, shape, jnp.float32) * scale
    return {
        "x": nrm(ks[0], (BATCH, SEQ, D_MODEL), 1.0),
        "mix_norm_g": 1.0 + nrm(ks[1], (DEPTH, D_MODEL), 0.02),
        "w_in": nrm(ks[2], (DEPTH, D_MODEL, IN_WIDTH), D_MODEL ** -0.5),
        "q_norm_g": 1.0 + nrm(ks[3], (DEPTH, HEAD_DIM), 0.02),
        "k_norm_g": 1.0 + nrm(ks[4], (DEPTH, HEAD_DIM), 0.02),
        "sinks": nrm(ks[5], (DEPTH, SWA_HEADS), 0.5),
        "w_branch_sb": nrm(ks[6], (DEPTH, SB_WIDTH, D_MODEL), SB_WIDTH ** -0.5),
        "w_branch_swa": nrm(ks[7], (DEPTH, SWA_Q_WIDTH, D_MODEL), SWA_Q_WIDTH ** -0.5),
        "w_out": nrm(ks[8], (DEPTH, D_MODEL, D_MODEL), D_MODEL ** -0.5),
        "mlp_norm_g": 1.0 + nrm(ks[9], (DEPTH, D_MODEL), 0.02),
        "w_up": nrm(ks[10], (DEPTH, D_MODEL, D_FF), D_MODEL ** -0.5),
        "w_down": nrm(ks[11], (DEPTH, D_FF, D_MODEL), D_FF ** -0.5),
    }


def reference(x, mix_norm_g, w_in, q_norm_g, k_norm_g, sinks, w_branch_sb, w_branch_swa,
              w_out, mlp_norm_g, w_up, w_down):
    B, S, D = x.shape
    cos, sin = rope_tables(S)
    split_at = np.cumsum([SB_WIDTH, SB_WIDTH, SB_WIDTH, SWA_Q_WIDTH, SWA_KV_WIDTH, SWA_KV_WIDTH]).tolist()
    for l in range(DEPTH):
        h = rms_norm(x, mix_norm_g[l])
        proj = h @ w_in[l]
        sb_q, sb_k, sb_v, sw_q, sw_k, sw_v, gate_logits = jnp.split(proj, split_at, axis=-1)

        to_heads = lambda t, n: t.reshape(B, S, n, HEAD_DIM)
        o_sb = stick_breaking_attention(to_heads(sb_q, SB_HEADS), to_heads(sb_k, SB_HEADS),
                                        to_heads(sb_v, SB_HEADS)).reshape(B, S, SB_WIDTH)
        y_sb = o_sb @ w_branch_sb[l]

        q = apply_rope(rms_norm(to_heads(sw_q, SWA_HEADS), q_norm_g[l]), cos, sin)
        k = apply_rope(rms_norm(to_heads(sw_k, SWA_KV_HEADS), k_norm_g[l]), cos, sin)
        v = to_heads(sw_v, SWA_KV_HEADS)
        y_swa = sliding_window_sink_attention(q, k, v, sinks[l]) @ w_branch_swa[l]

        gates = jax.nn.sigmoid(gate_logits.astype(jnp.float32)).astype(x.dtype).reshape(B, S, N_BRANCHES, D)
        merged = gates[:, :, 0] * y_sb + gates[:, :, 1] * y_swa
        x = x + merged @ w_out[l]

        h2 = rms_norm(x, mlp_norm_g[l])
        x = x + jnp.square(jax.nn.relu(h2 @ w_up[l])) @ w_down[l]
    return x
```

```python
import functools

import jax
import jax.numpy as jnp
import numpy as np
from jax import lax
from jax.experimental import pallas as pl
from jax.experimental.pallas import tpu as pltpu

D_MODEL = 1024
HEAD_DIM = 64
SB_HEADS = 8
SWA_HEADS = 8
SWA_KV_HEADS = 2
WINDOW = 128
D_FF = 4 * D_MODEL
ROPE_THETA = 10000.0
NORM_EPS = 1e-6
SB_WIDTH = SB_HEADS * HEAD_DIM
SWA_Q_WIDTH = SWA_HEADS * HEAD_DIM
SWA_KV_WIDTH = SWA_KV_HEADS * HEAD_DIM
SCALE = HEAD_DIM ** -0.5

LANES = 128
HEADS_PER_TILE = LANES // HEAD_DIM

COL_SB_Q = 0
COL_SB_K = COL_SB_Q + SB_WIDTH
COL_SB_V = COL_SB_K + SB_WIDTH
COL_SW_Q = COL_SB_V + SB_WIDTH
COL_SW_K = COL_SW_Q + SWA_Q_WIDTH
COL_SW_V = COL_SW_K + 2 * SWA_KV_WIDTH
QKV_WIDTH = COL_SW_V + 2 * SWA_KV_WIDTH

TOKEN_TILE = 512
SB_BLOCK = 128
FF_CHUNK = 512
VMEM_LIMIT = 56 * 1024 * 1024

F32 = jnp.float32
BF16 = jnp.bfloat16


def _dot(a, b):
    return jnp.dot(a, b, preferred_element_type=F32)


def _rms_norm_rows(x, g):
    ms = jnp.mean(x * x, axis=-1, keepdims=True)
    return x * lax.rsqrt(ms + NORM_EPS) * g


def _split_bf16(x):
    hi = x.astype(BF16)
    lo = (x - hi.astype(F32)).astype(BF16)
    return hi, lo


def _inproj_kernel(x_ref, g_ref, w_ref, qg_ref, kg_ref, cos_ref, sin_ref, hm_ref, out_ref):
    tm = x_ref.shape[0]
    h = _rms_norm_rows(x_ref[...], g_ref[...]).astype(BF16)

    def proj(c0, width):
        return _dot(h, w_ref[:, c0:c0 + width])

    out_ref[:, COL_SB_Q:COL_SB_K] = (proj(COL_SB_Q, SB_WIDTH) * SCALE).astype(BF16)
    out_ref[:, COL_SB_K:COL_SB_V] = proj(COL_SB_K, SB_WIDTH).astype(BF16)
    out_ref[:, COL_SB_V:COL_SW_Q] = proj(COL_SB_V, SB_WIDTH).astype(BF16)

    cos = cos_ref[...]
    sin = sin_ref[...]
    lane = lax.broadcasted_iota(jnp.int32, (tm, LANES), 1)
    first_half = (lane % HEAD_DIM) < (HEAD_DIM // 2)

    def norm_rope(y, gain):
        hi, lo = _split_bf16(y * y)
        ms = _dot(jnp.concatenate([hi, lo], axis=1), hm_ref[...])
        yn = y * lax.rsqrt(ms + NORM_EPS) * gain
        rot = jnp.where(first_half,
                        pltpu.roll(yn, LANES - HEAD_DIM // 2, 1),
                        pltpu.roll(yn, HEAD_DIM // 2, 1))
        return yn * cos + rot * sin

    q = proj(COL_SW_Q, SWA_Q_WIDTH)
    for c in range(SWA_Q_WIDTH // LANES):
        y = norm_rope(q[:, c * LANES:(c + 1) * LANES], qg_ref[...]) * SCALE
        out_ref[:, COL_SW_Q + c * LANES:COL_SW_Q + (c + 1) * LANES] = y.astype(BF16)
    k = proj(COL_SW_K, 2 * SWA_KV_WIDTH)
    for c in range(2 * SWA_KV_WIDTH // LANES):
        y = norm_rope(k[:, c * LANES:(c + 1) * LANES], kg_ref[...])
        out_ref[:, COL_SW_K + c * LANES:COL_SW_K + (c + 1) * LANES] = y.astype(BF16)
    out_ref[:, COL_SW_V:QKV_WIDTH] = proj(COL_SW_V, 2 * SWA_KV_WIDTH).astype(BF16)


def _inproj(x2, g, w_qkv, qg, kg, cos_t, sin_t, head_mean, seq):
    n = x2.shape[0]
    tm = min(TOKEN_TILE, seq)
    pos_blocks = seq // tm
    const = lambda shape: pl.BlockSpec(shape, lambda t: (0, 0))
    return pl.pallas_call(
        _inproj_kernel,
        out_shape=jax.ShapeDtypeStruct((n, QKV_WIDTH), BF16),
        grid=(n // tm,),
        in_specs=[
            pl.BlockSpec((tm, D_MODEL), lambda t: (t, 0)),
            const((1, D_MODEL)),
            const((D_MODEL, QKV_WIDTH)),
            const((1, LANES)),
            const((1, LANES)),
            pl.BlockSpec((tm, LANES), lambda t: (t % pos_blocks, 0)),
            pl.BlockSpec((tm, LANES), lambda t: (t % pos_blocks, 0)),
            const((2 * LANES, LANES)),
        ],
        out_specs=pl.BlockSpec((tm, QKV_WIDTH), lambda t: (t, 0)),
        compiler_params=pltpu.CompilerParams(
            dimension_semantics=("arbitrary",), vmem_limit_bytes=VMEM_LIMIT),
        name="norm_inproj",
    )(x2, g, w_qkv, qg, kg, cos_t, sin_t, head_mean)


def _sb_kernel(q_ref, k_ref, v_ref, o_ref, carry_ref, acc_ref):
    t = SB_BLOCK
    i = pl.program_id(2)
    q = q_ref[...]
    lane = lax.broadcasted_iota(jnp.int32, (t, LANES), 1)
    zero = jnp.zeros_like(q)
    qq = jnp.concatenate([jnp.where(lane < HEAD_DIM, q, zero),
                          jnp.where(lane >= HEAD_DIM, q, zero)], axis=0)

    r = lax.broadcasted_iota(jnp.int32, (2 * t, 2 * t), 0) % t
    c = lax.broadcasted_iota(jnp.int32, (2 * t, 2 * t), 1)
    cum = jnp.where((c >= t) | (r >= c), 1.0, 0.0).astype(BF16)

    row = lax.broadcasted_iota(jnp.int32, (2 * t, t), 0) % t
    col = lax.broadcasted_iota(jnp.int32, (2 * t, t), 1)
    strict = col < row

    carry_ref[...] = jnp.zeros_like(carry_ref)
    acc_ref[...] = jnp.zeros_like(acc_ref)

    def key_block(j, diagonal):
        start = pl.multiple_of(j * t, t)
        kb = k_ref[pl.ds(start, t), :]
        vb = v_ref[pl.ds(start, t), :]
        z = lax.dot_general(qq, kb, (((1,), (1,)), ((), ())), preferred_element_type=F32)
        log_keep = -(jnp.maximum(z, 0.0) + jnp.log(1.0 + jnp.exp(-jnp.abs(z))))
        if diagonal:
            log_keep = jnp.where(strict, log_keep, 0.0)
        hi, lo = _split_bf16(log_keep)
        cs = _dot(jnp.concatenate([hi, lo], axis=1), cum)
        w = jnp.exp(z + cs[:, :t] + carry_ref[...])
        if diagonal:
            w = jnp.where(strict, w, 0.0)
        acc_ref[...] += _dot(w.astype(BF16), vb)
        carry_ref[...] += cs[:, t:]

    key_block(i, True)

    def body(n, _):
        key_block(i - 1 - n, False)
        return 0

    lax.fori_loop(0, i, body, 0)

    acc = acc_ref[...]
    o_ref[...] = jnp.where(lane < HEAD_DIM, acc[:t], acc[t:]).astype(BF16)


def _sb_attention(qkv3):
    b, s, _ = qkv3.shape
    t = SB_BLOCK
    tiles = SB_WIDTH // LANES
    kq, kk, kv = COL_SB_Q // LANES, COL_SB_K // LANES, COL_SB_V // LANES
    return pl.pallas_call(
        _sb_kernel,
        out_shape=jax.ShapeDtypeStruct((b, s, SB_WIDTH), BF16),
        grid=(b, tiles, s // t),
        in_specs=[
            pl.BlockSpec((None, t, LANES), lambda bi, hp, i: (bi, i, kq + hp)),
            pl.BlockSpec((None, s, LANES), lambda bi, hp, i: (bi, 0, kk + hp)),
            pl.BlockSpec((None, s, LANES), lambda bi, hp, i: (bi, 0, kv + hp)),
        ],
        out_specs=pl.BlockSpec((None, t, LANES), lambda bi, hp, i: (bi, i, hp)),
        scratch_shapes=[pltpu.VMEM((2 * t, t), F32), pltpu.VMEM((2 * t, LANES), F32)],
        compiler_params=pltpu.CompilerParams(
            dimension_semantics=("arbitrary", "arbitrary", "arbitrary"),
            vmem_limit_bytes=VMEM_LIMIT),
        name="stick_breaking",
    )(qkv3, qkv3, qkv3)


def _swa_kernel(sink_ref, q_ref, kp_ref, kc_ref, vp_ref, vc_ref, o_ref):
    t = WINDOW
    n = pl.program_id(1)
    group = SWA_HEADS // SWA_KV_HEADS
    lane = lax.broadcasted_iota(jnp.int32, (t, LANES), 1)
    low = lane < HEAD_DIM

    i_idx = lax.broadcasted_iota(jnp.int32, (t, 2 * t), 0)
    j_idx = lax.broadcasted_iota(jnp.int32, (t, 2 * t), 1)
    rel = j_idx - t - i_idx
    valid = (rel <= 0) & (rel > -WINDOW) & (j_idx >= t - n * t)

    for g in range(SWA_KV_HEADS):
        kb = jnp.concatenate([kp_ref[:, g * LANES:(g + 1) * LANES],
                              kc_ref[:, g * LANES:(g + 1) * LANES]], axis=0)
        vb = jnp.concatenate([vp_ref[:, g * LANES:(g + 1) * LANES],
                              vc_ref[:, g * LANES:(g + 1) * LANES]], axis=0)
        for c in range(group // HEADS_PER_TILE):
            tile = g * (group // HEADS_PER_TILE) + c
            q = q_ref[:, tile * LANES:(tile + 1) * LANES]
            zero = jnp.zeros_like(q)
            outs = []
            for half in range(HEADS_PER_TILE):
                head = tile * HEADS_PER_TILE + half
                qh = jnp.where(low if half == 0 else ~low, q, zero)
                s = lax.dot_general(qh, kb, (((1,), (1,)), ((), ())),
                                    preferred_element_type=F32)
                s = jnp.where(valid, s, -jnp.inf)
                sink = sink_ref[head]
                m = jnp.maximum(jnp.max(s, axis=-1, keepdims=True), sink)
                p = jnp.exp(s - m)
                denom = jnp.sum(p, axis=-1, keepdims=True) + jnp.exp(sink - m)
                outs.append(_dot(p.astype(BF16), vb) / denom)
            o_ref[:, tile * LANES:(tile + 1) * LANES] = (
                jnp.where(low, outs[0], outs[1]).astype(BF16))


def _swa_attention(qkv3, sinks):
    b, s, _ = qkv3.shape
    t = WINDOW
    cq = COL_SW_Q // SWA_Q_WIDTH
    ck = COL_SW_K // (2 * SWA_KV_WIDTH)
    cv = COL_SW_V // (2 * SWA_KV_WIDTH)
    kv_w = 2 * SWA_KV_WIDTH
    prev = lambda n: jnp.maximum(n - 1, 0)
    return pl.pallas_call(
        _swa_kernel,
        out_shape=jax.ShapeDtypeStruct((b, s, SWA_Q_WIDTH), BF16),
        grid=(b, s // t),
        in_specs=[
            pl.BlockSpec(memory_space=pltpu.SMEM),
            pl.BlockSpec((None, t, SWA_Q_WIDTH), lambda bi, n: (bi, n, cq)),
            pl.BlockSpec((None, t, kv_w), lambda bi, n: (bi, prev(n), ck)),
            pl.BlockSpec((None, t, kv_w), lambda bi, n: (bi, n, ck)),
            pl.BlockSpec((None, t, kv_w), lambda bi, n: (bi, prev(n), cv)),
            pl.BlockSpec((None, t, kv_w), lambda bi, n: (bi, n, cv)),
        ],
        out_specs=pl.BlockSpec((None, t, SWA_Q_WIDTH), lambda bi, n: (bi, n, 0)),
        compiler_params=pltpu.CompilerParams(
            dimension_semantics=("arbitrary", "arbitrary"), vmem_limit_bytes=VMEM_LIMIT),
        name="sliding_window",
    )(sinks, qkv3, qkv3, qkv3, qkv3, qkv3)


def _merge_kernel(x_ref, osb_ref, osw_ref, g_ref, wg_ref, wsb_ref, wsw_ref, wo_ref, out_ref):
    x = x_ref[...]
    h = _rms_norm_rows(x, g_ref[...]).astype(BF16)
    osb = osb_ref[...]
    osw = osw_ref[...]
    chunk = 512
    merged = []
    for c0 in range(0, D_MODEL, chunk):
        y_sb = _dot(osb, wsb_ref[:, c0:c0 + chunk])
        y_sw = _dot(osw, wsw_ref[:, c0:c0 + chunk])
        g_sb = jax.nn.sigmoid(_dot(h, wg_ref[:, c0:c0 + chunk]))
        g_sw = jax.nn.sigmoid(_dot(h, wg_ref[:, D_MODEL + c0:D_MODEL + c0 + chunk]))
        merged.append((g_sb * y_sb + g_sw * y_sw).astype(BF16))
    merged = jnp.concatenate(merged, axis=1)
    for c0 in range(0, D_MODEL, chunk):
        out_ref[:, c0:c0 + chunk] = x[:, c0:c0 + chunk] + _dot(merged, wo_ref[:, c0:c0 + chunk])


def _merge_out(x2, o_sb, o_sw, g, w_gate, w_bsb, w_bsw, w_out):
    n = x2.shape[0]
    tm = min(TOKEN_TILE, n)
    const = lambda shape: pl.BlockSpec(shape, lambda t: (0, 0))
    row = lambda width: pl.BlockSpec((tm, width), lambda t: (t, 0))
    return pl.pallas_call(
        _merge_kernel,
        out_shape=jax.ShapeDtypeStruct((n, D_MODEL), F32),
        grid=(n // tm,),
        in_specs=[
            row(D_MODEL), row(SB_WIDTH), row(SWA_Q_WIDTH),
            const((1, D_MODEL)),
            const((D_MODEL, 2 * D_MODEL)),
            const((SB_WIDTH, D_MODEL)),
            const((SWA_Q_WIDTH, D_MODEL)),
            const((D_MODEL, D_MODEL)),
        ],
        out_specs=row(D_MODEL),
        compiler_params=pltpu.CompilerParams(
            dimension_semantics=("arbitrary",), vmem_limit_bytes=VMEM_LIMIT),
        name="merge_outproj",
    )(x2, o_sb, o_sw, g, w_gate, w_bsb, w_bsw, w_out)


def _mlp_kernel(x_ref, g_ref, wu_ref, wd_ref, out_ref):
    x = x_ref[...]
    h = _rms_norm_rows(x, g_ref[...]).astype(BF16)
    acc = x
    for c0 in range(0, D_FF, FF_CHUNK):
        u = jnp.maximum(_dot(h, wu_ref[:, c0:c0 + FF_CHUNK]), 0.0)
        acc = acc + _dot((u * u).astype(BF16), wd_ref[c0:c0 + FF_CHUNK, :])
    out_ref[...] = acc


def _mlp(x2, g, w_up, w_down):
    n = x2.shape[0]
    tm = min(TOKEN_TILE, n)
    resident = lambda shape: pl.BlockSpec(shape, lambda t: (0, 0), pipeline_mode=pl.Buffered(1))
    row = pl.BlockSpec((tm, D_MODEL), lambda t: (t, 0))
    return pl.pallas_call(
        _mlp_kernel,
        out_shape=jax.ShapeDtypeStruct((n, D_MODEL), F32),
        grid=(n // tm,),
        in_specs=[row, resident((1, D_MODEL)), resident((D_MODEL, D_FF)),
                  resident((D_FF, D_MODEL))],
        out_specs=row,
        compiler_params=pltpu.CompilerParams(
            dimension_semantics=("arbitrary",), vmem_limit_bytes=VMEM_LIMIT),
        name="relu2_mlp",
    )(x2, g, w_up, w_down)


def _rope_tables(seq):
    inv_freq = 1.0 / (ROPE_THETA ** (jnp.arange(0, HEAD_DIM, 2, dtype=F32) / HEAD_DIM))
    ang = jnp.arange(seq, dtype=F32)[:, None] * inv_freq[None, :]
    cos, sin = jnp.cos(ang), jnp.sin(ang)
    reps = LANES // HEAD_DIM
    cos_t = jnp.tile(jnp.concatenate([cos, cos], axis=1), (1, reps))
    sin_t = jnp.tile(jnp.concatenate([-sin, sin], axis=1), (1, reps))
    return cos_t, sin_t


def _head_mean_matrix():
    idx = np.arange(LANES) // HEAD_DIM
    m = (idx[:, None] == idx[None, :]).astype(np.float32) / HEAD_DIM
    return jnp.asarray(np.concatenate([m, m], axis=0), dtype=BF16)


def _twice_per_head(w):
    d = w.shape[0]
    w = w.reshape(d, SWA_KV_HEADS, 1, HEAD_DIM)
    return jnp.broadcast_to(w, (d, SWA_KV_HEADS, 2, HEAD_DIM)).reshape(d, 2 * SWA_KV_WIDTH)


@jax.jit
def _forward(x, mix_norm_g, w_in, q_norm_g, k_norm_g, sinks, w_branch_sb, w_branch_swa,
             w_out, mlp_norm_g, w_up, w_down):
    b, s, d = x.shape
    depth = w_in.shape[0]
    cos_t, sin_t = _rope_tables(s)
    head_mean = _head_mean_matrix()
    ref_kv = COL_SW_Q + SWA_Q_WIDTH
    x2 = x.reshape(b * s, d)
    for l in range(depth):
        w = w_in[l]
        w_qkv = jnp.concatenate(
            [w[:, :ref_kv],
             _twice_per_head(w[:, ref_kv:ref_kv + SWA_KV_WIDTH]),
             _twice_per_head(w[:, ref_kv + SWA_KV_WIDTH:ref_kv + 2 * SWA_KV_WIDTH])],
            axis=1).astype(BF16)
        w_gate = w[:, ref_kv + 2 * SWA_KV_WIDTH:].astype(BF16)
        qg = jnp.tile(q_norm_g[l], HEADS_PER_TILE)[None, :]
        kg = jnp.tile(k_norm_g[l], HEADS_PER_TILE)[None, :]

        qkv = _inproj(x2, mix_norm_g[l][None, :], w_qkv, qg, kg, cos_t, sin_t, head_mean, s)
        qkv3 = qkv.reshape(b, s, QKV_WIDTH)
        o_sb = _sb_attention(qkv3).reshape(b * s, SB_WIDTH)
        o_sw = _swa_attention(qkv3, sinks[l]).reshape(b * s, SWA_Q_WIDTH)
        x2 = _merge_out(x2, o_sb, o_sw, mix_norm_g[l][None, :], w_gate,
                        w_branch_sb[l].astype(BF16), w_branch_swa[l].astype(BF16),
                        w_out[l].astype(BF16))
        x2 = _mlp(x2, mlp_norm_g[l][None, :], w_up[l].astype(BF16), w_down[l].astype(BF16))
    return x2.reshape(b, s, d)


def kernel(x, mix_norm_g, w_in, q_norm_g, k_norm_g, sinks, w_branch_sb, w_branch_swa, w_out,
           mlp_norm_g, w_up, w_down):
    return _forward(x, mix_norm_g, w_in, q_norm_g, k_norm_g, sinks, w_branch_sb, w_branch_swa,
                    w_out, mlp_norm_g, w_up, w_down)
```

```python
import functools

import jax
import jax.numpy as jnp
import numpy as np
from jax import lax
from jax.experimental import pallas as pl
from jax.experimental.pallas import tpu as pltpu

D_MODEL = 1024
HEAD_DIM = 64
SB_HEADS = 8
SWA_HEADS = 8
SWA_KV_HEADS = 2
WINDOW = 128
D_FF = 4 * D_MODEL
ROPE_THETA = 10000.0
NORM_EPS = 1e-6
SB_WIDTH = SB_HEADS * HEAD_DIM
SWA_Q_WIDTH = SWA_HEADS * HEAD_DIM
SWA_KV_WIDTH = SWA_KV_HEADS * HEAD_DIM
SCALE = HEAD_DIM ** -0.5

LANES = 128
HEADS_PER_TILE = LANES // HEAD_DIM

COL_SB_Q = 0
COL_SB_K = COL_SB_Q + SB_WIDTH
COL_SB_V = COL_SB_K + SB_WIDTH
COL_SW_Q = COL_SB_V + SB_WIDTH
COL_SW_K = COL_SW_Q + SWA_Q_WIDTH
COL_SW_V = COL_SW_K + 2 * SWA_KV_WIDTH
QKV_WIDTH = COL_SW_V + 2 * SWA_KV_WIDTH

TOKEN_TILE = 512
SB_BLOCK = 128
SB_BAND = 3
SB_UNITS = 4
SB_SKIP_LOG = -120.0
FF_CHUNK = 512
VMEM_LIMIT = 56 * 1024 * 1024

F32 = jnp.float32
BF16 = jnp.bfloat16


def _dot(a, b):
    return jnp.dot(a, b, preferred_element_type=F32)


def _rms_norm_rows(x, g):
    ms = jnp.mean(x * x, axis=-1, keepdims=True)
    return x * lax.rsqrt(ms + NORM_EPS) * g


def _split_bf16(x):
    hi = x.astype(BF16)
    lo = (x - hi.astype(F32)).astype(BF16)
    return hi, lo


def _inproj_kernel(x_ref, g_ref, w_ref, qg_ref, kg_ref, cos_ref, sin_ref, hm_ref, out_ref):
    tm = x_ref.shape[0]
    h = _rms_norm_rows(x_ref[...], g_ref[...]).astype(BF16)

    def proj(c0, width):
        return _dot(h, w_ref[:, c0:c0 + width])

    out_ref[:, COL_SB_Q:COL_SB_K] = (proj(COL_SB_Q, SB_WIDTH) * -SCALE).astype(BF16)
    out_ref[:, COL_SB_K:COL_SB_V] = proj(COL_SB_K, SB_WIDTH).astype(BF16)
    out_ref[:, COL_SB_V:COL_SW_Q] = proj(COL_SB_V, SB_WIDTH).astype(BF16)

    cos = cos_ref[...]
    sin = sin_ref[...]
    lane = lax.broadcasted_iota(jnp.int32, (tm, LANES), 1)
    first_half = (lane % HEAD_DIM) < (HEAD_DIM // 2)

    def norm_rope(y, gain):
        hi, lo = _split_bf16(y * y)
        ms = _dot(jnp.concatenate([hi, lo], axis=1), hm_ref[...])
        yn = y * lax.rsqrt(ms + NORM_EPS) * gain
        rot = jnp.where(first_half,
                        pltpu.roll(yn, LANES - HEAD_DIM // 2, 1),
                        pltpu.roll(yn, HEAD_DIM // 2, 1))
        return yn * cos + rot * sin

    q = proj(COL_SW_Q, SWA_Q_WIDTH)
    for c in range(SWA_Q_WIDTH // LANES):
        y = norm_rope(q[:, c * LANES:(c + 1) * LANES], qg_ref[...]) * SCALE
        out_ref[:, COL_SW_Q + c * LANES:COL_SW_Q + (c + 1) * LANES] = y.astype(BF16)
    k = proj(COL_SW_K, 2 * SWA_KV_WIDTH)
    for c in range(2 * SWA_KV_WIDTH // LANES):
        y = norm_rope(k[:, c * LANES:(c + 1) * LANES], kg_ref[...])
        out_ref[:, COL_SW_K + c * LANES:COL_SW_K + (c + 1) * LANES] = y.astype(BF16)
    out_ref[:, COL_SW_V:QKV_WIDTH] = proj(COL_SW_V, 2 * SWA_KV_WIDTH).astype(BF16)


def _inproj(x2, g, w_qkv, qg, kg, cos_t, sin_t, head_mean, seq):
    n = x2.shape[0]
    tm = min(TOKEN_TILE, seq)
    pos_blocks = seq // tm
    const = lambda shape: pl.BlockSpec(shape, lambda t: (0, 0))
    return pl.pallas_call(
        _inproj_kernel,
        out_shape=jax.ShapeDtypeStruct((n, QKV_WIDTH), BF16),
        grid=(n // tm,),
        in_specs=[
            pl.BlockSpec((tm, D_MODEL), lambda t: (t, 0)),
            const((1, D_MODEL)),
            const((D_MODEL, QKV_WIDTH)),
            const((1, LANES)),
            const((1, LANES)),
            pl.BlockSpec((tm, LANES), lambda t: (t % pos_blocks, 0)),
            pl.BlockSpec((tm, LANES), lambda t: (t % pos_blocks, 0)),
            const((2 * LANES, LANES)),
        ],
        out_specs=pl.BlockSpec((tm, QKV_WIDTH), lambda t: (t, 0)),
        compiler_params=pltpu.CompilerParams(
            dimension_semantics=("arbitrary",), vmem_limit_bytes=VMEM_LIMIT),
        name="norm_inproj",
    )(x2, g, w_qkv, qg, kg, cos_t, sin_t, head_mean)


def _neg_abs(x):
    bits = lax.bitcast_convert_type(x, jnp.uint32) | jnp.uint32(0x80000000)
    return lax.bitcast_convert_type(bits, F32)


def _max_rows_to_tile(x):
    while x.shape[0] > 8:
        half = x.shape[0] // 2
        x = jnp.maximum(x[:half], x[half:])
    return x


def _sb_kernel(q_ref, k_ref, v_ref, cum_ref, o_ref, carry_ref, acc_ref, live_ref):
    t = SB_BLOCK
    units = q_ref.shape[1] // LANES
    i = pl.program_id(2)
    lane = lax.broadcasted_iota(jnp.int32, (t, LANES), 1)
    low = lane < HEAD_DIM

    row = lax.broadcasted_iota(jnp.int32, (2 * t, t), 0) % t
    col = lax.broadcasted_iota(jnp.int32, (2 * t, t), 1)
    strict = col < row

    def tile(u):
        return slice(u * LANES, (u + 1) * LANES)

    def stacked_q(u):
        q = q_ref[:, tile(u)]
        zero = jnp.zeros_like(q)
        return jnp.concatenate([jnp.where(low, q, zero), jnp.where(low, zero, q)], axis=0)

    def neg_scores(u, start, blocks):
        kb = k_ref[pl.ds(start, blocks * t), tile(u)]
        return lax.dot_general(stacked_q(u), kb, (((1,), (1,)), ((), ())),
                               preferred_element_type=F32)

    def block_weights(nz, carry, diagonal):
        log_keep = jnp.minimum(nz, 0.0) - jnp.log(1.0 + jnp.exp(_neg_abs(nz)))
        if diagonal:
            log_keep = jnp.where(strict, log_keep, 0.0)
        hi, lo = _split_bf16(log_keep)
        cs = _dot(jnp.concatenate([hi, lo], axis=1), cum_ref[...])
        log_w = cs[:, :t] - nz
        if carry is not None:
            log_w = log_w + carry
        w = jnp.exp(log_w)
        if diagonal:
            w = jnp.where(strict, w, 0.0)
        return w.astype(BF16), cs[:, t:]

    def set_live(carries):
        tiles = [_max_rows_to_tile(c) for c in carries]
        while len(tiles) > 1:
            tiles = [jnp.maximum(a, b) for a, b in zip(tiles[::2], tiles[1::2])]
        live_ref[0] = jnp.max(tiles[0])

    def key_block(j, diagonal):
        start = pl.multiple_of(j * t, t)
        carries = []
        for u in range(units):
            nz = neg_scores(u, start, 1)
            w, tot = block_weights(nz, None if diagonal else carry_ref[u], diagonal)
            pv = _dot(w, v_ref[pl.ds(start, t), tile(u)])
            if diagonal:
                acc_ref[u] = pv
            else:
                acc_ref[u] += pv
                tot = carry_ref[u] + tot
            carry_ref[u] = tot
            carries.append(tot)
        set_live(carries)

    @pl.when(i >= SB_BAND - 1)
    def _band():
        start = pl.multiple_of((i - (SB_BAND - 1)) * t, t)
        carries = []
        for u in range(units):
            nz = neg_scores(u, start, SB_BAND)
            carry = None
            ws = [None] * SB_BAND
            for blk in reversed(range(SB_BAND)):
                ws[blk], tot = block_weights(nz[:, blk * t:(blk + 1) * t], carry,
                                             blk == SB_BAND - 1)
                carry = tot if carry is None else carry + tot
            acc_ref[u] = _dot(jnp.concatenate(ws, axis=1),
                              v_ref[pl.ds(start, SB_BAND * t), tile(u)])
            carry_ref[u] = carry
            carries.append(carry)
        set_live(carries)

    @pl.when(i < SB_BAND - 1)
    def _edge():
        key_block(i, True)

    def cond(j):
        return (j >= 0) & (live_ref[0] > SB_SKIP_LOG)

    def body(j):
        key_block(j, False)
        return j - 1

    lax.while_loop(cond, body, jnp.where(i >= SB_BAND - 1, i - SB_BAND, i - 1))

    for u in range(units):
        acc = acc_ref[u]
        o_ref[:, tile(u)] = jnp.where(low, acc[:t], acc[t:]).astype(BF16)


def _sb_attention(qkv3):
    b, s, _ = qkv3.shape
    t = SB_BLOCK
    width = SB_UNITS * LANES
    kq, kk, kv = COL_SB_Q // width, COL_SB_K // width, COL_SB_V // width
    return pl.pallas_call(
        _sb_kernel,
        out_shape=jax.ShapeDtypeStruct((b, s, SB_WIDTH), BF16),
        grid=(b, SB_WIDTH // width, s // t),
        in_specs=[
            pl.BlockSpec((None, t, width), lambda bi, hp, i: (bi, i, kq + hp)),
            pl.BlockSpec((None, s, width), lambda bi, hp, i: (bi, 0, kk + hp)),
            pl.BlockSpec((None, s, width), lambda bi, hp, i: (bi, 0, kv + hp)),
            pl.BlockSpec((2 * t, 2 * t), lambda bi, hp, i: (0, 0)),
        ],
        out_specs=pl.BlockSpec((None, t, width), lambda bi, hp, i: (bi, i, hp)),
        scratch_shapes=[pltpu.VMEM((SB_UNITS, 2 * t, t), F32),
                        pltpu.VMEM((SB_UNITS, 2 * t, LANES), F32),
                        pltpu.SMEM((1,), F32)],
        compiler_params=pltpu.CompilerParams(
            dimension_semantics=("arbitrary", "arbitrary", "arbitrary"),
            vmem_limit_bytes=VMEM_LIMIT),
        name="stick_breaking",
    )(qkv3, qkv3, qkv3, _cumsum_matrix(t))


def _cumsum_matrix(t):
    r = np.arange(2 * t)[:, None] % t
    c = np.arange(2 * t)[None, :]
    return jnp.asarray((c >= t) | (r >= c), dtype=BF16)


def _swa_kernel(sink_ref, q_ref, kp_ref, kc_ref, vp_ref, vc_ref, o_ref):
    t = WINDOW
    n = pl.program_id(1)
    group = SWA_HEADS // SWA_KV_HEADS
    lane = lax.broadcasted_iota(jnp.int32, (t, LANES), 1)
    low = lane < HEAD_DIM

    i_idx = lax.broadcasted_iota(jnp.int32, (t, 2 * t), 0)
    j_idx = lax.broadcasted_iota(jnp.int32, (t, 2 * t), 1)
    rel = j_idx - t - i_idx
    valid = (rel <= 0) & (rel > -WINDOW) & (j_idx >= t - n * t)

    for g in range(SWA_KV_HEADS):
        kb = jnp.concatenate([kp_ref[:, g * LANES:(g + 1) * LANES],
                              kc_ref[:, g * LANES:(g + 1) * LANES]], axis=0)
        vb = jnp.concatenate([vp_ref[:, g * LANES:(g + 1) * LANES],
                              vc_ref[:, g * LANES:(g + 1) * LANES]], axis=0)
        for c in range(group // HEADS_PER_TILE):
            tile = g * (group // HEADS_PER_TILE) + c
            q = q_ref[:, tile * LANES:(tile + 1) * LANES]
            zero = jnp.zeros_like(q)
            outs = []
            for half in range(HEADS_PER_TILE):
                head = tile * HEADS_PER_TILE + half
                qh = jnp.where(low if half == 0 else ~low, q, zero)
                s = lax.dot_general(qh, kb, (((1,), (1,)), ((), ())),
                                    preferred_element_type=F32)
                s = jnp.where(valid, s, -jnp.inf)
                sink = sink_ref[head]
                m = jnp.maximum(jnp.max(s, axis=-1, keepdims=True), sink)
                p = jnp.exp(s - m)
                denom = jnp.sum(p, axis=-1, keepdims=True) + jnp.exp(sink - m)
                outs.append(_dot(p.astype(BF16), vb) / denom)
            o_ref[:, tile * LANES:(tile + 1) * LANES] = (
                jnp.where(low, outs[0], outs[1]).astype(BF16))


def _swa_attention(qkv3, sinks):
    b, s, _ = qkv3.shape
    t = WINDOW
    cq = COL_SW_Q // SWA_Q_WIDTH
    ck = COL_SW_K // (2 * SWA_KV_WIDTH)
    cv = COL_SW_V // (2 * SWA_KV_WIDTH)
    kv_w = 2 * SWA_KV_WIDTH
    prev = lambda n: jnp.maximum(n - 1, 0)
    return pl.pallas_call(
        _swa_kernel,
        out_shape=jax.ShapeDtypeStruct((b, s, SWA_Q_WIDTH), BF16),
        grid=(b, s // t),
        in_specs=[
            pl.BlockSpec(memory_space=pltpu.SMEM),
            pl.BlockSpec((None, t, SWA_Q_WIDTH), lambda bi, n: (bi, n, cq)),
            pl.BlockSpec((None, t, kv_w), lambda bi, n: (bi, prev(n), ck)),
            pl.BlockSpec((None, t, kv_w), lambda bi, n: (bi, n, ck)),
            pl.BlockSpec((None, t, kv_w), lambda bi, n: (bi, prev(n), cv)),
            pl.BlockSpec((None, t, kv_w), lambda bi, n: (bi, n, cv)),
        ],
        out_specs=pl.BlockSpec((None, t, SWA_Q_WIDTH), lambda bi, n: (bi, n, 0)),
        compiler_params=pltpu.CompilerParams(
            dimension_semantics=("arbitrary", "arbitrary"), vmem_limit_bytes=VMEM_LIMIT),
        name="sliding_window",
    )(sinks, qkv3, qkv3, qkv3, qkv3, qkv3)


def _merge_kernel(x_ref, osb_ref, osw_ref, g_ref, wg_ref, wsb_ref, wsw_ref, wo_ref, out_ref):
    x = x_ref[...]
    h = _rms_norm_rows(x, g_ref[...]).astype(BF16)
    osb = osb_ref[...]
    osw = osw_ref[...]
    chunk = 512
    merged = []
    for c0 in range(0, D_MODEL, chunk):
        y_sb = _dot(osb, wsb_ref[:, c0:c0 + chunk])
        y_sw = _dot(osw, wsw_ref[:, c0:c0 + chunk])
        g_sb = jax.nn.sigmoid(_dot(h, wg_ref[:, c0:c0 + chunk]))
        g_sw = jax.nn.sigmoid(_dot(h, wg_ref[:, D_MODEL + c0:D_MODEL + c0 + chunk]))
        merged.append((g_sb * y_sb + g_sw * y_sw).astype(BF16))
    merged = jnp.concatenate(merged, axis=1)
    for c0 in range(0, D_MODEL, chunk):
        out_ref[:, c0:c0 + chunk] = x[:, c0:c0 + chunk] + _dot(merged, wo_ref[:, c0:c0 + chunk])


def _merge_out(x2, o_sb, o_sw, g, w_gate, w_bsb, w_bsw, w_out):
    n = x2.shape[0]
    tm = min(TOKEN_TILE, n)
    const = lambda shape: pl.BlockSpec(shape, lambda t: (0, 0))
    row = lambda width: pl.BlockSpec((tm, width), lambda t: (t, 0))
    return pl.pallas_call(
        _merge_kernel,
        out_shape=jax.ShapeDtypeStruct((n, D_MODEL), F32),
        grid=(n // tm,),
        in_specs=[
            row(D_MODEL), row(SB_WIDTH), row(SWA_Q_WIDTH),
            const((1, D_MODEL)),
            const((D_MODEL, 2 * D_MODEL)),
            const((SB_WIDTH, D_MODEL)),
            const((SWA_Q_WIDTH, D_MODEL)),
            const((D_MODEL, D_MODEL)),
        ],
        out_specs=row(D_MODEL),
        compiler_params=pltpu.CompilerParams(
            dimension_semantics=("arbitrary",), vmem_limit_bytes=VMEM_LIMIT),
        name="merge_outproj",
    )(x2, o_sb, o_sw, g, w_gate, w_bsb, w_bsw, w_out)


def _mlp_kernel(x_ref, g_ref, wu_ref, wd_ref, out_ref):
    x = x_ref[...]
    h = _rms_norm_rows(x, g_ref[...]).astype(BF16)
    acc = x
    for c0 in range(0, D_FF, FF_CHUNK):
        u = jnp.maximum(_dot(h, wu_ref[:, c0:c0 + FF_CHUNK]), 0.0)
        acc = acc + _dot((u * u).astype(BF16), wd_ref[c0:c0 + FF_CHUNK, :])
    out_ref[...] = acc


def _mlp(x2, g, w_up, w_down):
    n = x2.shape[0]
    tm = min(TOKEN_TILE, n)
    resident = lambda shape: pl.BlockSpec(shape, lambda t: (0, 0), pipeline_mode=pl.Buffered(1))
    row = pl.BlockSpec((tm, D_MODEL), lambda t: (t, 0))
    return pl.pallas_call(
        _mlp_kernel,
        out_shape=jax.ShapeDtypeStruct((n, D_MODEL), F32),
        grid=(n // tm,),
        in_specs=[row, resident((1, D_MODEL)), resident((D_MODEL, D_FF)),
                  resident((D_FF, D_MODEL))],
        out_specs=row,
        compiler_params=pltpu.CompilerParams(
            dimension_semantics=("arbitrary",), vmem_limit_bytes=VMEM_LIMIT),
        name="relu2_mlp",
    )(x2, g, w_up, w_down)


def _rope_tables(seq):
    inv_freq = 1.0 / (ROPE_THETA ** (jnp.arange(0, HEAD_DIM, 2, dtype=F32) / HEAD_DIM))
    ang = jnp.arange(seq, dtype=F32)[:, None] * inv_freq[None, :]
    cos, sin = jnp.cos(ang), jnp.sin(ang)
    reps = LANES // HEAD_DIM
    cos_t = jnp.tile(jnp.concatenate([cos, cos], axis=1), (1, reps))
    sin_t = jnp.tile(jnp.concatenate([-sin, sin], axis=1), (1, reps))
    return cos_t, sin_t


def _head_mean_matrix():
    idx = np.arange(LANES) // HEAD_DIM
    m = (idx[:, None] == idx[None, :]).astype(np.float32) / HEAD_DIM
    return jnp.asarray(np.concatenate([m, m], axis=0), dtype=BF16)


def _twice_per_head(w):
    d = w.shape[0]
    w = w.reshape(d, SWA_KV_HEADS, 1, HEAD_DIM)
    return jnp.broadcast_to(w, (d, SWA_KV_HEADS, 2, HEAD_DIM)).reshape(d, 2 * SWA_KV_WIDTH)


@jax.jit
def _forward(x, mix_norm_g, w_in, q_norm_g, k_norm_g, sinks, w_branch_sb, w_branch_swa,
             w_out, mlp_norm_g, w_up, w_down):
    b, s, d = x.shape
    depth = w_in.shape[0]
    cos_t, sin_t = _rope_tables(s)
    head_mean = _head_mean_matrix()
    ref_kv = COL_SW_Q + SWA_Q_WIDTH
    x2 = x.reshape(b * s, d)
    for l in range(depth):
        w = w_in[l]
        w_qkv = jnp.concatenate(
            [w[:, :ref_kv],
             _twice_per_head(w[:, ref_kv:ref_kv + SWA_KV_WIDTH]),
             _twice_per_head(w[:, ref_kv + SWA_KV_WIDTH:ref_kv + 2 * SWA_KV_WIDTH])],
            axis=1).astype(BF16)
        w_gate = w[:, ref_kv + 2 * SWA_KV_WIDTH:].astype(BF16)
        qg = jnp.tile(q_norm_g[l], HEADS_PER_TILE)[None, :]
        kg = jnp.tile(k_norm_g[l], HEADS_PER_TILE)[None, :]

        qkv = _inproj(x2, mix_norm_g[l][None, :], w_qkv, qg, kg, cos_t, sin_t, head_mean, s)
        qkv3 = qkv.reshape(b, s, QKV_WIDTH)
        o_sb = _sb_attention(qkv3).reshape(b * s, SB_WIDTH)
        o_sw = _swa_attention(qkv3, sinks[l]).reshape(b * s, SWA_Q_WIDTH)
        x2 = _merge_out(x2, o_sb, o_sw, mix_norm_g[l][None, :], w_gate,
                        w_branch_sb[l].astype(BF16), w_branch_swa[l].astype(BF16),
                        w_out[l].astype(BF16))
        x2 = _mlp(x2, mlp_norm_g[l][None, :], w_up[l].astype(BF16), w_down[l].astype(BF16))
    return x2.reshape(b, s, d)


def kernel(x, mix_norm_g, w_in, q_norm_g, k_norm_g, sinks, w_branch_sb, w_branch_swa, w_out,
           mlp_norm_g, w_up, w_down):
    return _forward(x, mix_norm_g, w_in, q_norm_g, k_norm_g, sinks, w_branch_sb, w_branch_swa,
                    w_out, mlp_norm_g, w_up, w_down)
```

```python
import jax
import jax.numpy as jnp
import numpy as np
from jax import lax
from jax.experimental import pallas as pl
from jax.experimental.pallas import tpu as pltpu

D_MODEL = 1024
HEAD_DIM = 64
SB_HEADS = 8
SWA_HEADS = 8
SWA_KV_HEADS = 2
WINDOW = 128
D_FF = 4 * D_MODEL
ROPE_THETA = 10000.0
NORM_EPS = 1e-6
SB_WIDTH = SB_HEADS * HEAD_DIM
SWA_Q_WIDTH = SWA_HEADS * HEAD_DIM
SWA_KV_WIDTH = SWA_KV_HEADS * HEAD_DIM
SCALE = HEAD_DIM ** -0.5

LANES = 128
HEADS_PER_TILE = LANES // HEAD_DIM

COL_SB_Q = 0
COL_SB_K = COL_SB_Q + SB_WIDTH
COL_SB_V = COL_SB_K + SB_WIDTH
COL_SW_Q = COL_SB_V + SB_WIDTH
COL_SW_K = COL_SW_Q + SWA_Q_WIDTH
COL_SW_V = COL_SW_K + 2 * SWA_KV_WIDTH
QKV_WIDTH = COL_SW_V + 2 * SWA_KV_WIDTH

TOKEN_TILE = 512
SB_BLOCK = 128
SB_BAND = 4
SB_UNITS = SB_WIDTH // LANES
SB_SKIP_LOG = -120.0
FF_CHUNK = 512
VMEM_LIMIT = 56 * 1024 * 1024

F32 = jnp.float32
BF16 = jnp.bfloat16


def _dot(a, b):
    return jnp.dot(a, b, preferred_element_type=F32)


def _rms_norm_rows(x, g):
    ms = jnp.mean(x * x, axis=-1, keepdims=True)
    return x * lax.rsqrt(ms + NORM_EPS) * g


def _split_bf16(x):
    hi = x.astype(BF16)
    lo = (x - hi.astype(F32)).astype(BF16)
    return hi, lo


def _inproj_kernel(x_ref, g_ref, w_ref, qg_ref, kg_ref, cos_ref, sin_ref, hm_ref, out_ref):
    tm = x_ref.shape[0]
    h = _rms_norm_rows(x_ref[...], g_ref[...]).astype(BF16)

    def proj(c0, width):
        return _dot(h, w_ref[:, c0:c0 + width])

    out_ref[:, COL_SB_Q:COL_SB_K] = (proj(COL_SB_Q, SB_WIDTH) * -SCALE).astype(BF16)
    out_ref[:, COL_SB_K:COL_SB_V] = proj(COL_SB_K, SB_WIDTH).astype(BF16)
    out_ref[:, COL_SB_V:COL_SW_Q] = proj(COL_SB_V, SB_WIDTH).astype(BF16)

    cos = cos_ref[...]
    sin = sin_ref[...]
    lane = lax.broadcasted_iota(jnp.int32, (tm, LANES), 1)
    first_half = (lane % HEAD_DIM) < (HEAD_DIM // 2)

    def norm_rope(y, gain):
        hi, lo = _split_bf16(y * y)
        ms = _dot(jnp.concatenate([hi, lo], axis=1), hm_ref[...])
        yn = y * lax.rsqrt(ms + NORM_EPS) * gain
        rot = jnp.where(first_half,
                        pltpu.roll(yn, LANES - HEAD_DIM // 2, 1),
                        pltpu.roll(yn, HEAD_DIM // 2, 1))
        return yn * cos + rot * sin

    q = proj(COL_SW_Q, SWA_Q_WIDTH)
    for c in range(SWA_Q_WIDTH // LANES):
        y = norm_rope(q[:, c * LANES:(c + 1) * LANES], qg_ref[...]) * SCALE
        out_ref[:, COL_SW_Q + c * LANES:COL_SW_Q + (c + 1) * LANES] = y.astype(BF16)
    k = proj(COL_SW_K, 2 * SWA_KV_WIDTH)
    for c in range(2 * SWA_KV_WIDTH // LANES):
        y = norm_rope(k[:, c * LANES:(c + 1) * LANES], kg_ref[...])
        out_ref[:, COL_SW_K + c * LANES:COL_SW_K + (c + 1) * LANES] = y.astype(BF16)
    out_ref[:, COL_SW_V:QKV_WIDTH] = proj(COL_SW_V, 2 * SWA_KV_WIDTH).astype(BF16)


def _inproj(x2, g, w_qkv, qg, kg, cos_t, sin_t, head_mean, seq):
    n = x2.shape[0]
    tm = min(TOKEN_TILE, seq)
    pos_blocks = seq // tm
    const = lambda shape: pl.BlockSpec(shape, lambda t: (0, 0))
    return pl.pallas_call(
        _inproj_kernel,
        out_shape=jax.ShapeDtypeStruct((n, QKV_WIDTH), BF16),
        grid=(n // tm,),
        in_specs=[
            pl.BlockSpec((tm, D_MODEL), lambda t: (t, 0)),
            const((1, D_MODEL)),
            const((D_MODEL, QKV_WIDTH)),
            const((1, LANES)),
            const((1, LANES)),
            pl.BlockSpec((tm, LANES), lambda t: (t % pos_blocks, 0)),
            pl.BlockSpec((tm, LANES), lambda t: (t % pos_blocks, 0)),
            const((2 * LANES, LANES)),
        ],
        out_specs=pl.BlockSpec((tm, QKV_WIDTH), lambda t: (t, 0)),
        compiler_params=pltpu.CompilerParams(
            dimension_semantics=("arbitrary",), vmem_limit_bytes=VMEM_LIMIT),
        name="norm_inproj",
    )(x2, g, w_qkv, qg, kg, cos_t, sin_t, head_mean)


def _neg_abs(x):
    bits = lax.bitcast_convert_type(x, jnp.uint32) | jnp.uint32(0x80000000)
    return lax.bitcast_convert_type(bits, F32)


def _max_rows_to_tile(x):
    while x.shape[0] > 8:
        half = x.shape[0] // 2
        x = jnp.maximum(x[:half], x[half:])
    return x


def _swa_block(n, sink_ref, q_ref, kp_ref, kc_ref, vp_ref, vc_ref, o_ref):
    t = WINDOW
    group = SWA_HEADS // SWA_KV_HEADS
    lane = lax.broadcasted_iota(jnp.int32, (t, LANES), 1)
    low = lane < HEAD_DIM

    i_idx = lax.broadcasted_iota(jnp.int32, (t, 2 * t), 0)
    j_idx = lax.broadcasted_iota(jnp.int32, (t, 2 * t), 1)
    rel = j_idx - t - i_idx
    valid = (rel <= 0) & (rel > -WINDOW) & (j_idx >= t - n * t)

    for g in range(SWA_KV_HEADS):
        kb = jnp.concatenate([kp_ref[:, g * LANES:(g + 1) * LANES],
                              kc_ref[:, g * LANES:(g + 1) * LANES]], axis=0)
        vb = jnp.concatenate([vp_ref[:, g * LANES:(g + 1) * LANES],
                              vc_ref[:, g * LANES:(g + 1) * LANES]], axis=0)
        for c in range(group // HEADS_PER_TILE):
            tile = g * (group // HEADS_PER_TILE) + c
            q = q_ref[:, tile * LANES:(tile + 1) * LANES]
            zero = jnp.zeros_like(q)
            outs = []
            for half in range(HEADS_PER_TILE):
                head = tile * HEADS_PER_TILE + half
                qh = jnp.where(low if half == 0 else ~low, q, zero)
                s = lax.dot_general(qh, kb, (((1,), (1,)), ((), ())),
                                    preferred_element_type=F32)
                s = jnp.where(valid, s, -jnp.inf)
                sink = sink_ref[head]
                m = jnp.maximum(jnp.max(s, axis=-1, keepdims=True), sink)
                p = jnp.exp(s - m)
                denom = jnp.sum(p, axis=-1, keepdims=True) + jnp.exp(sink - m)
                outs.append(_dot(p.astype(BF16), vb) / denom)
            o_ref[:, tile * LANES:(tile + 1) * LANES] = (
                jnp.where(low, outs[0], outs[1]).astype(BF16))


def _mixers_kernel(sink_ref, q_ref, k_ref, v_ref, cum_ref, swq_ref, kp_ref, kc_ref, vp_ref,
                   vc_ref, o_ref, osw_ref, carry_ref, acc_ref, live_ref):
    t = SB_BLOCK
    units = q_ref.shape[1] // LANES
    i = pl.program_id(1)
    lane = lax.broadcasted_iota(jnp.int32, (t, LANES), 1)
    low = lane < HEAD_DIM

    row = lax.broadcasted_iota(jnp.int32, (2 * t, t), 0) % t
    col = lax.broadcasted_iota(jnp.int32, (2 * t, t), 1)
    strict = col < row

    def tile(u):
        return slice(u * LANES, (u + 1) * LANES)

    def stacked_q(u):
        q = q_ref[:, tile(u)]
        zero = jnp.zeros_like(q)
        return jnp.concatenate([jnp.where(low, q, zero), jnp.where(low, zero, q)], axis=0)

    def neg_scores(u, start, blocks):
        kb = k_ref[pl.ds(start, blocks * t), tile(u)]
        return lax.dot_general(stacked_q(u), kb, (((1,), (1,)), ((), ())),
                               preferred_element_type=F32)

    def block_weights(nz, carry, diagonal):
        log_keep = jnp.minimum(nz, 0.0) - jnp.log(1.0 + jnp.exp(_neg_abs(nz)))
        if diagonal:
            log_keep = jnp.where(strict, log_keep, 0.0)
        cs = _dot(log_keep.astype(BF16), cum_ref[...])
        log_w = cs[:, :t] - nz
        if carry is not None:
            log_w = log_w + carry
        w = jnp.exp(log_w)
        if diagonal:
            w = jnp.where(strict, w, 0.0)
        return w.astype(BF16), cs[:, t:]

    def set_live(carries):
        tiles = [_max_rows_to_tile(c) for c in carries]
        while len(tiles) > 1:
            tiles = [jnp.maximum(a, b) for a, b in zip(tiles[::2], tiles[1::2])]
        live_ref[0] = jnp.max(tiles[0])

    def key_block(j, diagonal):
        start = pl.multiple_of(j * t, t)
        carries = []
        for u in range(units):
            nz = neg_scores(u, start, 1)
            w, tot = block_weights(nz, None if diagonal else carry_ref[u], diagonal)
            pv = _dot(w, v_ref[pl.ds(start, t), tile(u)])
            if diagonal:
                acc_ref[u] = pv
            else:
                acc_ref[u] += pv
                tot = carry_ref[u] + tot
            carry_ref[u] = tot
            carries.append(tot)
        set_live(carries)

    def sliding_window():
        _swa_block(i, sink_ref, swq_ref, kp_ref, kc_ref, vp_ref, vc_ref, osw_ref)

    @pl.when(i >= SB_BAND - 1)
    def _band():
        start = pl.multiple_of((i - (SB_BAND - 1)) * t, t)
        carries = []
        for u in range(units):
            nz = neg_scores(u, start, SB_BAND)
            carry = None
            ws = [None] * SB_BAND
            for blk in reversed(range(SB_BAND)):
                ws[blk], tot = block_weights(nz[:, blk * t:(blk + 1) * t], carry,
                                             blk == SB_BAND - 1)
                carry = tot if carry is None else carry + tot
            acc_ref[u] = _dot(jnp.concatenate(ws, axis=1),
                              v_ref[pl.ds(start, SB_BAND * t), tile(u)])
            carry_ref[u] = carry
            carries.append(carry)
        set_live(carries)
        sliding_window()

    @pl.when(i < SB_BAND - 1)
    def _edge():
        key_block(i, True)
        sliding_window()

    def cond(j):
        return (j >= 0) & (live_ref[0] > SB_SKIP_LOG)

    def body(j):
        key_block(j, False)
        return j - 1

    lax.while_loop(cond, body, jnp.where(i >= SB_BAND - 1, i - SB_BAND, i - 1))

    for u in range(units):
        acc = acc_ref[u]
        o_ref[:, tile(u)] = jnp.where(low, acc[:t], acc[t:]).astype(BF16)


def _cumsum_matrix(t):
    r = np.arange(t)[:, None]
    c = np.arange(2 * t)[None, :]
    return jnp.asarray((c >= t) | (r >= c), dtype=BF16)


def _token_mixers(qkv3, sinks):
    b, s, _ = qkv3.shape
    t = SB_BLOCK
    assert WINDOW == t
    kq, kk, kv = COL_SB_Q // SB_WIDTH, COL_SB_K // SB_WIDTH, COL_SB_V // SB_WIDTH
    kv_w = 2 * SWA_KV_WIDTH
    cq, ck, cv = COL_SW_Q // SWA_Q_WIDTH, COL_SW_K // kv_w, COL_SW_V // kv_w
    prev = lambda i: jnp.maximum(i - 1, 0)
    out = jax.ShapeDtypeStruct((b, s, SB_WIDTH), BF16)
    out_spec = pl.BlockSpec((None, t, SB_WIDTH), lambda bi, i: (bi, i, 0))
    return pl.pallas_call(
        _mixers_kernel,
        out_shape=(out, out),
        grid=(b, s // t),
        in_specs=[
            pl.BlockSpec(memory_space=pltpu.SMEM),
            pl.BlockSpec((None, t, SB_WIDTH), lambda bi, i: (bi, i, kq)),
            pl.BlockSpec((None, s, SB_WIDTH), lambda bi, i: (bi, 0, kk)),
            pl.BlockSpec((None, s, SB_WIDTH), lambda bi, i: (bi, 0, kv)),
            pl.BlockSpec((t, 2 * t), lambda bi, i: (0, 0)),
            pl.BlockSpec((None, t, SWA_Q_WIDTH), lambda bi, i: (bi, i, cq)),
            pl.BlockSpec((None, t, kv_w), lambda bi, i: (bi, prev(i), ck)),
            pl.BlockSpec((None, t, kv_w), lambda bi, i: (bi, i, ck)),
            pl.BlockSpec((None, t, kv_w), lambda bi, i: (bi, prev(i), cv)),
            pl.BlockSpec((None, t, kv_w), lambda bi, i: (bi, i, cv)),
        ],
        out_specs=(out_spec, out_spec),
        scratch_shapes=[pltpu.VMEM((SB_UNITS, 2 * t, t), F32),
                        pltpu.VMEM((SB_UNITS, 2 * t, LANES), F32),
                        pltpu.SMEM((1,), F32)],
        compiler_params=pltpu.CompilerParams(
            dimension_semantics=("arbitrary", "arbitrary"), vmem_limit_bytes=VMEM_LIMIT),
        name="token_mixers",
    )(sinks, qkv3, qkv3, qkv3, _cumsum_matrix(t), qkv3, qkv3, qkv3, qkv3, qkv3)


def _merge_kernel(x_ref, osb_ref, osw_ref, g_ref, wg_ref, wsb_ref, wsw_ref, wo_ref, out_ref):
    x = x_ref[...]
    h = _rms_norm_rows(x, g_ref[...]).astype(BF16)
    osb = osb_ref[...]
    osw = osw_ref[...]
    chunk = 512
    merged = []
    for c0 in range(0, D_MODEL, chunk):
        y_sb = _dot(osb, wsb_ref[:, c0:c0 + chunk])
        y_sw = _dot(osw, wsw_ref[:, c0:c0 + chunk])
        g_sb = jax.nn.sigmoid(_dot(h, wg_ref[:, c0:c0 + chunk]))
        g_sw = jax.nn.sigmoid(_dot(h, wg_ref[:, D_MODEL + c0:D_MODEL + c0 + chunk]))
        merged.append((g_sb * y_sb + g_sw * y_sw).astype(BF16))
    merged = jnp.concatenate(merged, axis=1)
    for c0 in range(0, D_MODEL, chunk):
        out_ref[:, c0:c0 + chunk] = x[:, c0:c0 + chunk] + _dot(merged, wo_ref[:, c0:c0 + chunk])


def _merge_out(x2, o_sb, o_sw, g, w_gate, w_bsb, w_bsw, w_out):
    n = x2.shape[0]
    tm = min(TOKEN_TILE, n)
    const = lambda shape: pl.BlockSpec(shape, lambda t: (0, 0))
    row = lambda width: pl.BlockSpec((tm, width), lambda t: (t, 0))
    return pl.pallas_call(
        _merge_kernel,
        out_shape=jax.ShapeDtypeStruct((n, D_MODEL), F32),
        grid=(n // tm,),
        in_specs=[
            row(D_MODEL), row(SB_WIDTH), row(SWA_Q_WIDTH),
            const((1, D_MODEL)),
            const((D_MODEL, 2 * D_MODEL)),
            const((SB_WIDTH, D_MODEL)),
            const((SWA_Q_WIDTH, D_MODEL)),
            const((D_MODEL, D_MODEL)),
        ],
        out_specs=row(D_MODEL),
        compiler_params=pltpu.CompilerParams(
            dimension_semantics=("arbitrary",), vmem_limit_bytes=VMEM_LIMIT),
        name="merge_outproj",
    )(x2, o_sb, o_sw, g, w_gate, w_bsb, w_bsw, w_out)


def _mlp_kernel(x_ref, g_ref, wu_ref, wd_ref, out_ref):
    x = x_ref[...]
    h = _rms_norm_rows(x, g_ref[...]).astype(BF16)
    acc = x
    for c0 in range(0, D_FF, FF_CHUNK):
        u = jnp.maximum(_dot(h, wu_ref[:, c0:c0 + FF_CHUNK]), 0.0)
        acc = acc + _dot((u * u).astype(BF16), wd_ref[c0:c0 + FF_CHUNK, :])
    out_ref[...] = acc


def _mlp(x2, g, w_up, w_down):
    n = x2.shape[0]
    tm = min(TOKEN_TILE, n)
    resident = lambda shape: pl.BlockSpec(shape, lambda t: (0, 0), pipeline_mode=pl.Buffered(1))
    row = pl.BlockSpec((tm, D_MODEL), lambda t: (t, 0))
    return pl.pallas_call(
        _mlp_kernel,
        out_shape=jax.ShapeDtypeStruct((n, D_MODEL), F32),
        grid=(n // tm,),
        in_specs=[row, resident((1, D_MODEL)), resident((D_MODEL, D_FF)),
                  resident((D_FF, D_MODEL))],
        out_specs=row,
        compiler_params=pltpu.CompilerParams(
            dimension_semantics=("arbitrary",), vmem_limit_bytes=VMEM_LIMIT),
        name="relu2_mlp",
    )(x2, g, w_up, w_down)


def _rope_tables(seq):
    inv_freq = 1.0 / (ROPE_THETA ** (jnp.arange(0, HEAD_DIM, 2, dtype=F32) / HEAD_DIM))
    ang = jnp.arange(seq, dtype=F32)[:, None] * inv_freq[None, :]
    cos, sin = jnp.cos(ang), jnp.sin(ang)
    reps = LANES // HEAD_DIM
    cos_t = jnp.tile(jnp.concatenate([cos, cos], axis=1), (1, reps))
    sin_t = jnp.tile(jnp.concatenate([-sin, sin], axis=1), (1, reps))
    return cos_t, sin_t


def _head_mean_matrix():
    idx = np.arange(LANES) // HEAD_DIM
    m = (idx[:, None] == idx[None, :]).astype(np.float32) / HEAD_DIM
    return jnp.asarray(np.concatenate([m, m], axis=0), dtype=BF16)


def _twice_per_head(w):
    d = w.shape[0]
    w = w.reshape(d, SWA_KV_HEADS, 1, HEAD_DIM)
    return jnp.broadcast_to(w, (d, SWA_KV_HEADS, 2, HEAD_DIM)).reshape(d, 2 * SWA_KV_WIDTH)


@jax.jit
def _forward(x, mix_norm_g, w_in, q_norm_g, k_norm_g, sinks, w_branch_sb, w_branch_swa,
             w_out, mlp_norm_g, w_up, w_down):
    b, s, d = x.shape
    depth = w_in.shape[0]
    cos_t, sin_t = _rope_tables(s)
    head_mean = _head_mean_matrix()
    ref_kv = COL_SW_Q + SWA_Q_WIDTH
    x2 = x.reshape(b * s, d)
    for l in range(depth):
        w = w_in[l]
        w_qkv = jnp.concatenate(
            [w[:, :ref_kv],
             _twice_per_head(w[:, ref_kv:ref_kv + SWA_KV_WIDTH]),
             _twice_per_head(w[:, ref_kv + SWA_KV_WIDTH:ref_kv + 2 * SWA_KV_WIDTH])],
            axis=1).astype(BF16)
        w_gate = w[:, ref_kv + 2 * SWA_KV_WIDTH:].astype(BF16)
        qg = jnp.tile(q_norm_g[l], HEADS_PER_TILE)[None, :]
        kg = jnp.tile(k_norm_g[l], HEADS_PER_TILE)[None, :]

        qkv = _inproj(x2, mix_norm_g[l][None, :], w_qkv, qg, kg, cos_t, sin_t, head_mean, s)
        o_sb, o_sw = _token_mixers(qkv.reshape(b, s, QKV_WIDTH), sinks[l])
        x2 = _merge_out(x2, o_sb.reshape(b * s, SB_WIDTH), o_sw.reshape(b * s, SWA_Q_WIDTH),
                        mix_norm_g[l][None, :], w_gate,
                        w_branch_sb[l].astype(BF16), w_branch_swa[l].astype(BF16),
                        w_out[l].astype(BF16))
        x2 = _mlp(x2, mlp_norm_g[l][None, :], w_up[l].astype(BF16), w_down[l].astype(BF16))
    return x2.reshape(b, s, d)


def kernel(x, mix_norm_g, w_in, q_norm_g, k_norm_g, sinks, w_branch_sb, w_branch_swa, w_out,
           mlp_norm_g, w_up, w_down):
    return _forward(x, mix_norm_g, w_in, q_norm_g, k_norm_g, sinks, w_branch_sb, w_branch_swa,
                    w_out, mlp_norm_g, w_up, w_down)
```

```python
import jax
import jax.numpy as jnp
import numpy as np
from jax import lax
from jax.experimental import pallas as pl
from jax.experimental.pallas import tpu as pltpu

D_MODEL = 1024
HEAD_DIM = 64
SB_HEADS = 8
SWA_HEADS = 8
SWA_KV_HEADS = 2
WINDOW = 128
D_FF = 4 * D_MODEL
ROPE_THETA = 10000.0
NORM_EPS = 1e-6
SB_WIDTH = SB_HEADS * HEAD_DIM
SWA_Q_WIDTH = SWA_HEADS * HEAD_DIM
SWA_KV_WIDTH = SWA_KV_HEADS * HEAD_DIM
SCALE = HEAD_DIM ** -0.5
LOG2_E = 1.4426950408889634

LANES = 128
HEADS_PER_TILE = LANES // HEAD_DIM

COL_SB_Q = 0
COL_SB_K = COL_SB_Q + SB_WIDTH
COL_SB_V = COL_SB_K + SB_WIDTH
COL_SW_Q = COL_SB_V + SB_WIDTH
COL_SW_K = COL_SW_Q + SWA_Q_WIDTH
COL_SW_V = COL_SW_K + 2 * SWA_KV_WIDTH
QKV_WIDTH = COL_SW_V + 2 * SWA_KV_WIDTH
W_COL_KV = COL_SW_K
W_QKV_WIDTH = W_COL_KV + 2 * SWA_KV_WIDTH

TOKEN_TILE = 512
SB_BLOCK = 128
SB_BAND = 4
SB_UNITS = SB_WIDTH // LANES
SB_SKIP_LOG = -120.0
SB_MASKED = 1e30
FF_CHUNK = 512
VMEM_LIMIT = 56 * 1024 * 1024

F32 = jnp.float32
BF16 = jnp.bfloat16


def _dot(a, b):
    return jnp.dot(a, b, preferred_element_type=F32)


def _rms_norm_rows(x, g):
    ms = jnp.mean(x * x, axis=-1, keepdims=True)
    return x * lax.rsqrt(ms + NORM_EPS) * g


def _split_bf16(x):
    hi = x.astype(BF16)
    lo = (x - hi.astype(F32)).astype(BF16)
    return hi, lo


def _inproj_kernel(x_ref, g_ref, w_ref, qg_ref, kg_ref, cos_ref, sin_ref, hm_ref, out_ref):
    tm = x_ref.shape[0]
    h = _rms_norm_rows(x_ref[...], g_ref[...]).astype(BF16)

    def proj(c0, width):
        return _dot(h, w_ref[:, c0:c0 + width])

    out_ref[:, COL_SB_Q:COL_SB_K] = (proj(COL_SB_Q, SB_WIDTH) * -SCALE).astype(BF16)
    out_ref[:, COL_SB_K:COL_SB_V] = proj(COL_SB_K, SB_WIDTH).astype(BF16)
    out_ref[:, COL_SB_V:COL_SW_Q] = proj(COL_SB_V, SB_WIDTH).astype(BF16)

    cos = cos_ref[...]
    sin = sin_ref[...]
    lane = lax.broadcasted_iota(jnp.int32, (tm, LANES), 1)
    first_half = (lane % HEAD_DIM) < (HEAD_DIM // 2)

    def norm_rope(y, gain):
        hi, lo = _split_bf16(y * y)
        ms = _dot(jnp.concatenate([hi, lo], axis=1), hm_ref[...])
        yn = y * lax.rsqrt(ms + NORM_EPS) * gain
        rot = jnp.where(first_half,
                        pltpu.roll(yn, LANES - HEAD_DIM // 2, 1),
                        pltpu.roll(yn, HEAD_DIM // 2, 1))
        return yn * cos + rot * sin

    q = proj(COL_SW_Q, SWA_Q_WIDTH)
    for c in range(SWA_Q_WIDTH // LANES):
        y = norm_rope(q[:, c * LANES:(c + 1) * LANES], qg_ref[...]) * SCALE
        out_ref[:, COL_SW_Q + c * LANES:COL_SW_Q + (c + 1) * LANES] = y.astype(BF16)
    low = lane < HEAD_DIM
    kv = proj(W_COL_KV, 2 * SWA_KV_WIDTH)
    for col, y in ((COL_SW_K, norm_rope(kv[:, :LANES], kg_ref[...])), (COL_SW_V, kv[:, LANES:])):
        swapped = pltpu.roll(y, HEAD_DIM, 1)
        out_ref[:, col:col + LANES] = jnp.where(low, y, swapped).astype(BF16)
        out_ref[:, col + LANES:col + 2 * LANES] = jnp.where(low, swapped, y).astype(BF16)


def _inproj(x2, g, w_qkv, qg, kg, cos_t, sin_t, head_mean, seq):
    n = x2.shape[0]
    tm = min(TOKEN_TILE, seq)
    pos_blocks = seq // tm
    const = lambda shape: pl.BlockSpec(shape, lambda t: (0, 0))
    return pl.pallas_call(
        _inproj_kernel,
        out_shape=jax.ShapeDtypeStruct((n, QKV_WIDTH), BF16),
        grid=(n // tm,),
        in_specs=[
            pl.BlockSpec((tm, D_MODEL), lambda t: (t, 0)),
            const((1, D_MODEL)),
            const((D_MODEL, W_QKV_WIDTH)),
            const((1, LANES)),
            const((1, LANES)),
            pl.BlockSpec((tm, LANES), lambda t: (t % pos_blocks, 0)),
            pl.BlockSpec((tm, LANES), lambda t: (t % pos_blocks, 0)),
            const((2 * LANES, LANES)),
        ],
        out_specs=pl.BlockSpec((tm, QKV_WIDTH), lambda t: (t, 0)),
        compiler_params=pltpu.CompilerParams(
            dimension_semantics=("arbitrary",), vmem_limit_bytes=VMEM_LIMIT),
        name="norm_inproj",
    )(x2, g, w_qkv, qg, kg, cos_t, sin_t, head_mean)


def _max_rows_to_tile(x):
    while x.shape[0] > 8:
        half = x.shape[0] // 2
        x = jnp.maximum(x[:half], x[half:])
    return x


def _swa_block(n, sink_ref, q_ref, kp_ref, kc_ref, vp_ref, vc_ref, o_ref):
    t = WINDOW
    group = SWA_HEADS // SWA_KV_HEADS
    lane = lax.broadcasted_iota(jnp.int32, (t, LANES), 1)
    low = lane < HEAD_DIM

    i_idx = lax.broadcasted_iota(jnp.int32, (t, 2 * t), 0)
    j_idx = lax.broadcasted_iota(jnp.int32, (t, 2 * t), 1)
    rel = j_idx - t - i_idx
    valid = (rel <= 0) & (rel > -WINDOW) & (j_idx >= t - n * t)

    for g in range(SWA_KV_HEADS):
        kb = jnp.concatenate([kp_ref[:, g * LANES:(g + 1) * LANES],
                              kc_ref[:, g * LANES:(g + 1) * LANES]], axis=0)
        vb = jnp.concatenate([vp_ref[:, g * LANES:(g + 1) * LANES],
                              vc_ref[:, g * LANES:(g + 1) * LANES]], axis=0)
        for c in range(group // HEADS_PER_TILE):
            tile = g * (group // HEADS_PER_TILE) + c
            q = q_ref[:, tile * LANES:(tile + 1) * LANES]
            zero = jnp.zeros_like(q)
            outs = []
            for half in range(HEADS_PER_TILE):
                head = tile * HEADS_PER_TILE + half
                qh = jnp.where(low if half == 0 else ~low, q, zero)
                s = lax.dot_general(qh, kb, (((1,), (1,)), ((), ())),
                                    preferred_element_type=F32)
                s = jnp.where(valid, s, -jnp.inf)
                sink = sink_ref[head]
                m = jnp.maximum(jnp.max(s, axis=-1, keepdims=True), sink)
                p = jnp.exp(s - m)
                denom = jnp.sum(p, axis=-1, keepdims=True) + jnp.exp(sink - m)
                outs.append(_dot(p.astype(BF16), vb) / denom)
            o_ref[:, tile * LANES:(tile + 1) * LANES] = (
                jnp.where(low, outs[0], outs[1]).astype(BF16))


def _mixers_kernel(sink_ref, q_ref, k_ref, v_ref, cum_ref, swq_ref, kp_ref, kc_ref, vp_ref,
                   vc_ref, o_ref, osw_ref, carry_ref, acc_ref, live_ref):
    t = SB_BLOCK
    units = q_ref.shape[1] // LANES
    i = pl.program_id(1)
    lane = lax.broadcasted_iota(jnp.int32, (t, LANES), 1)
    low = lane < HEAD_DIM

    row = lax.broadcasted_iota(jnp.int32, (2 * t, t), 0) % t
    col = lax.broadcasted_iota(jnp.int32, (2 * t, t), 1)
    strict = col < row

    def tile(u):
        return slice(u * LANES, (u + 1) * LANES)

    def stacked_q(u):
        q = q_ref[:, tile(u)]
        zero = jnp.zeros_like(q)
        return jnp.concatenate([jnp.where(low, q, zero), jnp.where(low, zero, q)], axis=0)

    def neg_scores(u, start, blocks):
        kb = k_ref[pl.ds(start, blocks * t), tile(u)]
        return lax.dot_general(stacked_q(u), kb, (((1,), (1,)), ((), ())),
                               preferred_element_type=F32)

    def block_weights(nz, carry, diagonal):
        if diagonal:
            nz = jnp.where(strict, nz, SB_MASKED)
        log_keep = jnp.minimum(nz, 0.0) - jnp.log(1.0 + jnp.exp2(jnp.abs(nz) * -LOG2_E))
        cs = _dot(log_keep.astype(BF16), cum_ref[...])
        log_w = cs[:, :t] - nz
        if carry is not None:
            log_w = log_w + carry
        return jnp.exp(log_w).astype(BF16), cs[:, t:]

    def set_live(carries):
        tiles = [_max_rows_to_tile(c) for c in carries]
        while len(tiles) > 1:
            tiles = [jnp.maximum(a, b) for a, b in zip(tiles[::2], tiles[1::2])]
        live_ref[0] = jnp.max(tiles[0])

    def key_block(j, diagonal):
        start = pl.multiple_of(j * t, t)
        carries = []
        for u in range(units):
            nz = neg_scores(u, start, 1)
            w, tot = block_weights(nz, None if diagonal else carry_ref[u], diagonal)
            pv = _dot(w, v_ref[pl.ds(start, t), tile(u)])
            if diagonal:
                acc_ref[u] = pv
            else:
                acc_ref[u] += pv
                tot = carry_ref[u] + tot
            carry_ref[u] = tot
            carries.append(tot)
        set_live(carries)

    def sliding_window():
        _swa_block(i, sink_ref, swq_ref, kp_ref, kc_ref, vp_ref, vc_ref, osw_ref)

    @pl.when(i >= SB_BAND - 1)
    def _band():
        start = pl.multiple_of((i - (SB_BAND - 1)) * t, t)
        carries = []
        for u in range(units):
            nz = neg_scores(u, start, SB_BAND)
            carry = None
            ws = [None] * SB_BAND
            for blk in reversed(range(SB_BAND)):
                ws[blk], tot = block_weights(nz[:, blk * t:(blk + 1) * t], carry,
                                             blk == SB_BAND - 1)
                carry = tot if carry is None else carry + tot
            acc_ref[u] = _dot(jnp.concatenate(ws, axis=1),
                              v_ref[pl.ds(start, SB_BAND * t), tile(u)])
            carry_ref[u] = carry
            carries.append(carry)
        set_live(carries)
        sliding_window()

    @pl.when(i < SB_BAND - 1)
    def _edge():
        key_block(i, True)
        sliding_window()

    def cond(j):
        return (j >= 0) & (live_ref[0] > SB_SKIP_LOG)

    def body(j):
        key_block(j, False)
        return j - 1

    lax.while_loop(cond, body, jnp.where(i >= SB_BAND - 1, i - SB_BAND, i - 1))

    for u in range(units):
        acc = acc_ref[u]
        o_ref[:, tile(u)] = jnp.where(low, acc[:t], acc[t:]).astype(BF16)


def _cumsum_matrix(t):
    r = np.arange(t)[:, None]
    c = np.arange(2 * t)[None, :]
    return jnp.asarray((c >= t) | (r >= c), dtype=BF16)


def _token_mixers(qkv3, sinks):
    b, s, _ = qkv3.shape
    t = SB_BLOCK
    assert WINDOW == t
    kq, kk, kv = COL_SB_Q // SB_WIDTH, COL_SB_K // SB_WIDTH, COL_SB_V // SB_WIDTH
    kv_w = 2 * SWA_KV_WIDTH
    cq, ck, cv = COL_SW_Q // SWA_Q_WIDTH, COL_SW_K // kv_w, COL_SW_V // kv_w
    prev = lambda i: jnp.maximum(i - 1, 0)
    out = jax.ShapeDtypeStruct((b, s, SB_WIDTH), BF16)
    out_spec = pl.BlockSpec((None, t, SB_WIDTH), lambda bi, i: (bi, i, 0))
    return pl.pallas_call(
        _mixers_kernel,
        out_shape=(out, out),
        grid=(b, s // t),
        in_specs=[
            pl.BlockSpec(memory_space=pltpu.SMEM),
            pl.BlockSpec((None, t, SB_WIDTH), lambda bi, i: (bi, i, kq)),
            pl.BlockSpec((None, s, SB_WIDTH), lambda bi, i: (bi, 0, kk)),
            pl.BlockSpec((None, s, SB_WIDTH), lambda bi, i: (bi, 0, kv)),
            pl.BlockSpec((t, 2 * t), lambda bi, i: (0, 0)),
            pl.BlockSpec((None, t, SWA_Q_WIDTH), lambda bi, i: (bi, i, cq)),
            pl.BlockSpec((None, t, kv_w), lambda bi, i: (bi, prev(i), ck)),
            pl.BlockSpec((None, t, kv_w), lambda bi, i: (bi, i, ck)),
            pl.BlockSpec((None, t, kv_w), lambda bi, i: (bi, prev(i), cv)),
            pl.BlockSpec((None, t, kv_w), lambda bi, i: (bi, i, cv)),
        ],
        out_specs=(out_spec, out_spec),
        scratch_shapes=[pltpu.VMEM((SB_UNITS, 2 * t, t), F32),
                        pltpu.VMEM((SB_UNITS, 2 * t, LANES), F32),
                        pltpu.SMEM((1,), F32)],
        compiler_params=pltpu.CompilerParams(
            dimension_semantics=("arbitrary", "arbitrary"), vmem_limit_bytes=VMEM_LIMIT),
        name="token_mixers",
    )(sinks, qkv3, qkv3, qkv3, _cumsum_matrix(t), qkv3, qkv3, qkv3, qkv3, qkv3)


def _merge_kernel(x_ref, osb_ref, osw_ref, g_ref, wg_ref, wsb_ref, wsw_ref, wo_ref, out_ref):
    x = x_ref[...]
    h = _rms_norm_rows(x, g_ref[...]).astype(BF16)
    osb = osb_ref[...]
    osw = osw_ref[...]
    chunk = 512
    merged = []
    for c0 in range(0, D_MODEL, chunk):
        y_sb = _dot(osb, wsb_ref[:, c0:c0 + chunk])
        y_sw = _dot(osw, wsw_ref[:, c0:c0 + chunk])
        g_sb = jax.nn.sigmoid(_dot(h, wg_ref[:, c0:c0 + chunk]))
        g_sw = jax.nn.sigmoid(_dot(h, wg_ref[:, D_MODEL + c0:D_MODEL + c0 + chunk]))
        merged.append((g_sb * y_sb + g_sw * y_sw).astype(BF16))
    merged = jnp.concatenate(merged, axis=1)
    for c0 in range(0, D_MODEL, chunk):
        out_ref[:, c0:c0 + chunk] = x[:, c0:c0 + chunk] + _dot(merged, wo_ref[:, c0:c0 + chunk])


def _merge_out(x2, o_sb, o_sw, g, w_gate, w_bsb, w_bsw, w_out):
    n = x2.shape[0]
    tm = min(TOKEN_TILE, n)
    const = lambda shape: pl.BlockSpec(shape, lambda t: (0, 0))
    row = lambda width: pl.BlockSpec((tm, width), lambda t: (t, 0))
    return pl.pallas_call(
        _merge_kernel,
        out_shape=jax.ShapeDtypeStruct((n, D_MODEL), F32),
        grid=(n // tm,),
        in_specs=[
            row(D_MODEL), row(SB_WIDTH), row(SWA_Q_WIDTH),
            const((1, D_MODEL)),
            const((D_MODEL, 2 * D_MODEL)),
            const((SB_WIDTH, D_MODEL)),
            const((SWA_Q_WIDTH, D_MODEL)),
            const((D_MODEL, D_MODEL)),
        ],
        out_specs=row(D_MODEL),
        compiler_params=pltpu.CompilerParams(
            dimension_semantics=("arbitrary",), vmem_limit_bytes=VMEM_LIMIT),
        name="merge_outproj",
    )(x2, o_sb, o_sw, g, w_gate, w_bsb, w_bsw, w_out)


def _mlp_kernel(x_ref, g_ref, wu_ref, wd_ref, out_ref):
    x = x_ref[...]
    h = _rms_norm_rows(x, g_ref[...]).astype(BF16)
    acc = x
    for c0 in range(0, D_FF, FF_CHUNK):
        u = jnp.maximum(_dot(h, wu_ref[:, c0:c0 + FF_CHUNK]), 0.0)
        acc = acc + _dot((u * u).astype(BF16), wd_ref[c0:c0 + FF_CHUNK, :])
    out_ref[...] = acc


def _mlp(x2, g, w_up, w_down):
    n = x2.shape[0]
    tm = min(TOKEN_TILE, n)
    resident = lambda shape: pl.BlockSpec(shape, lambda t: (0, 0), pipeline_mode=pl.Buffered(1))
    row = pl.BlockSpec((tm, D_MODEL), lambda t: (t, 0))
    return pl.pallas_call(
        _mlp_kernel,
        out_shape=jax.ShapeDtypeStruct((n, D_MODEL), F32),
        grid=(n // tm,),
        in_specs=[row, resident((1, D_MODEL)), resident((D_MODEL, D_FF)),
                  resident((D_FF, D_MODEL))],
        out_specs=row,
        compiler_params=pltpu.CompilerParams(
            dimension_semantics=("arbitrary",), vmem_limit_bytes=VMEM_LIMIT),
        name="relu2_mlp",
    )(x2, g, w_up, w_down)


def _rope_tables(seq):
    inv_freq = 1.0 / (ROPE_THETA ** (jnp.arange(0, HEAD_DIM, 2, dtype=F32) / HEAD_DIM))
    ang = jnp.arange(seq, dtype=F32)[:, None] * inv_freq[None, :]
    cos, sin = jnp.cos(ang), jnp.sin(ang)
    reps = LANES // HEAD_DIM
    cos_t = jnp.tile(jnp.concatenate([cos, cos], axis=1), (1, reps))
    sin_t = jnp.tile(jnp.concatenate([-sin, sin], axis=1), (1, reps))
    return cos_t, sin_t


def _head_mean_matrix():
    idx = np.arange(LANES) // HEAD_DIM
    m = (idx[:, None] == idx[None, :]).astype(np.float32) / HEAD_DIM
    return jnp.asarray(np.concatenate([m, m], axis=0), dtype=BF16)


@jax.jit
def _forward(x, mix_norm_g, w_in, q_norm_g, k_norm_g, sinks, w_branch_sb, w_branch_swa,
             w_out, mlp_norm_g, w_up, w_down):
    b, s, d = x.shape
    depth = w_in.shape[0]
    cos_t, sin_t = _rope_tables(s)
    head_mean = _head_mean_matrix()
    x2 = x.reshape(b * s, d)
    for l in range(depth):
        w_qkv = w_in[l][:, :W_QKV_WIDTH].astype(BF16)
        w_gate = w_in[l][:, W_QKV_WIDTH:].astype(BF16)
        qg = jnp.tile(q_norm_g[l], HEADS_PER_TILE)[None, :]
        kg = jnp.tile(k_norm_g[l], HEADS_PER_TILE)[None, :]

        qkv = _inproj(x2, mix_norm_g[l][None, :], w_qkv, qg, kg, cos_t, sin_t, head_mean, s)
        o_sb, o_sw = _token_mixers(qkv.reshape(b, s, QKV_WIDTH), sinks[l])
        x2 = _merge_out(x2, o_sb.reshape(b * s, SB_WIDTH), o_sw.reshape(b * s, SWA_Q_WIDTH),
                        mix_norm_g[l][None, :], w_gate,
                        w_branch_sb[l].astype(BF16), w_branch_swa[l].astype(BF16),
                        w_out[l].astype(BF16))
        x2 = _mlp(x2, mlp_norm_g[l][None, :], w_up[l].astype(BF16), w_down[l].astype(BF16))
    return x2.reshape(b, s, d)


def kernel(x, mix_norm_g, w_in, q_norm_g, k_norm_g, sinks, w_branch_sb, w_branch_swa, w_out,
           mlp_norm_g, w_up, w_down):
    return _forward(x, mix_norm_g, w_in, q_norm_g, k_norm_g, sinks, w_branch_sb, w_branch_swa,
                    w_out, mlp_norm_g, w_up, w_down)
```

```python
import jax
import jax.numpy as jnp
import numpy as np
from jax import lax
from jax.experimental import pallas as pl
from jax.experimental.pallas import tpu as pltpu

D_MODEL = 1024
HEAD_DIM = 64
SB_HEADS = 8
SWA_HEADS = 8
SWA_KV_HEADS = 2
WINDOW = 128
D_FF = 4 * D_MODEL
ROPE_THETA = 10000.0
NORM_EPS = 1e-6
SB_WIDTH = SB_HEADS * HEAD_DIM
SWA_Q_WIDTH = SWA_HEADS * HEAD_DIM
SWA_KV_WIDTH = SWA_KV_HEADS * HEAD_DIM
SCALE = HEAD_DIM ** -0.5
LOG2_E = 1.4426950408889634

LANES = 128
HEADS_PER_TILE = LANES // HEAD_DIM

COL_SB_Q = 0
COL_SB_K = COL_SB_Q + SB_WIDTH
COL_SB_V = COL_SB_K + SB_WIDTH
COL_SW_Q = COL_SB_V + SB_WIDTH
COL_SW_K = COL_SW_Q + SWA_Q_WIDTH
COL_SW_V = COL_SW_K + 2 * SWA_KV_WIDTH
QKV_WIDTH = COL_SW_V + 2 * SWA_KV_WIDTH
W_COL_KV = COL_SW_K
W_QKV_WIDTH = W_COL_KV + 2 * SWA_KV_WIDTH

TOKEN_TILE = 1024
SB_BLOCK = 128
SB_BAND = 4
SB_UNITS = SB_WIDTH // LANES
SB_SKIP_LOG = -120.0
SB_MASKED = 1e30
FF_CHUNK = 512
VMEM_LIMIT = 56 * 1024 * 1024

F32 = jnp.float32
BF16 = jnp.bfloat16


def _dot(a, b):
    return jnp.dot(a, b, preferred_element_type=F32)


def _rms_norm_rows(x, g):
    ms = jnp.mean(x * x, axis=-1, keepdims=True)
    return x * lax.rsqrt(ms + NORM_EPS) * g


def _split_bf16(x):
    hi = x.astype(BF16)
    lo = (x - hi.astype(F32)).astype(BF16)
    return hi, lo


def _inproj_kernel(x_ref, g_ref, w_ref, qg_ref, kg_ref, cos_ref, sin_ref, hm_ref, out_ref):
    tm = x_ref.shape[0]
    h = _rms_norm_rows(x_ref[...], g_ref[...]).astype(BF16)

    def proj(c0, width):
        return _dot(h, w_ref[:, c0:c0 + width])

    cos = cos_ref[...]
    sin = sin_ref[...]
    lane = lax.broadcasted_iota(jnp.int32, (tm, LANES), 1)
    first_half = (lane % HEAD_DIM) < (HEAD_DIM // 2)

    def norm_rope(y, gain):
        hi, lo = _split_bf16(y * y)
        ms = _dot(jnp.concatenate([hi, lo], axis=1), hm_ref[...])
        yn = y * lax.rsqrt(ms + NORM_EPS) * gain
        rot = jnp.where(first_half,
                        pltpu.roll(yn, LANES - HEAD_DIM // 2, 1),
                        pltpu.roll(yn, HEAD_DIM // 2, 1))
        return yn * cos + rot * sin

    q = proj(COL_SW_Q, SWA_Q_WIDTH)
    for c in range(SWA_Q_WIDTH // LANES):
        y = norm_rope(q[:, c * LANES:(c + 1) * LANES], qg_ref[...]) * SCALE
        out_ref[:, COL_SW_Q + c * LANES:COL_SW_Q + (c + 1) * LANES] = y.astype(BF16)
    low = lane < HEAD_DIM
    kv = proj(W_COL_KV, 2 * SWA_KV_WIDTH)
    for col, y in ((COL_SW_K, norm_rope(kv[:, :LANES], kg_ref[...])), (COL_SW_V, kv[:, LANES:])):
        swapped = pltpu.roll(y, HEAD_DIM, 1)
        out_ref[:, col:col + LANES] = jnp.where(low, y, swapped).astype(BF16)
        out_ref[:, col + LANES:col + 2 * LANES] = jnp.where(low, swapped, y).astype(BF16)

    out_ref[:, COL_SB_Q:COL_SB_K] = (proj(COL_SB_Q, SB_WIDTH) * -SCALE).astype(BF16)
    out_ref[:, COL_SB_K:COL_SB_V] = proj(COL_SB_K, SB_WIDTH).astype(BF16)
    out_ref[:, COL_SB_V:COL_SW_Q] = proj(COL_SB_V, SB_WIDTH).astype(BF16)


def _inproj(x2, g, w_qkv, qg, kg, cos_t, sin_t, head_mean, seq):
    n = x2.shape[0]
    tm = min(TOKEN_TILE, seq)
    pos_blocks = seq // tm
    const = lambda shape: pl.BlockSpec(shape, lambda t: (0, 0))
    return pl.pallas_call(
        _inproj_kernel,
        out_shape=jax.ShapeDtypeStruct((n, QKV_WIDTH), BF16),
        grid=(n // tm,),
        in_specs=[
            pl.BlockSpec((tm, D_MODEL), lambda t: (t, 0)),
            const((1, D_MODEL)),
            const((D_MODEL, W_QKV_WIDTH)),
            const((1, LANES)),
            const((1, LANES)),
            pl.BlockSpec((tm, LANES), lambda t: (t % pos_blocks, 0)),
            pl.BlockSpec((tm, LANES), lambda t: (t % pos_blocks, 0)),
            const((2 * LANES, LANES)),
        ],
        out_specs=pl.BlockSpec((tm, QKV_WIDTH), lambda t: (t, 0)),
        compiler_params=pltpu.CompilerParams(
            dimension_semantics=("arbitrary",), vmem_limit_bytes=VMEM_LIMIT),
        name="norm_inproj",
    )(x2, g, w_qkv, qg, kg, cos_t, sin_t, head_mean)


def _max_rows_to_tile(x):
    while x.shape[0] > 8:
        half = x.shape[0] // 2
        x = jnp.maximum(x[:half], x[half:])
    return x


def _swa_block(n, sink_ref, q_ref, kp_ref, kc_ref, vp_ref, vc_ref, o_ref):
    t = WINDOW
    group = SWA_HEADS // SWA_KV_HEADS
    lane = lax.broadcasted_iota(jnp.int32, (t, LANES), 1)
    low = lane < HEAD_DIM

    i_idx = lax.broadcasted_iota(jnp.int32, (t, 2 * t), 0)
    j_idx = lax.broadcasted_iota(jnp.int32, (t, 2 * t), 1)
    rel = j_idx - t - i_idx
    valid = (rel <= 0) & (rel > -WINDOW) & (j_idx >= t - n * t)

    for g in range(SWA_KV_HEADS):
        kb = jnp.concatenate([kp_ref[:, g * LANES:(g + 1) * LANES],
                              kc_ref[:, g * LANES:(g + 1) * LANES]], axis=0)
        vb = jnp.concatenate([vp_ref[:, g * LANES:(g + 1) * LANES],
                              vc_ref[:, g * LANES:(g + 1) * LANES]], axis=0)
        for c in range(group // HEADS_PER_TILE):
            tile = g * (group // HEADS_PER_TILE) + c
            q = q_ref[:, tile * LANES:(tile + 1) * LANES]
            zero = jnp.zeros_like(q)
            outs = []
            for half in range(HEADS_PER_TILE):
                head = tile * HEADS_PER_TILE + half
                qh = jnp.where(low if half == 0 else ~low, q, zero)
                s = lax.dot_general(qh, kb, (((1,), (1,)), ((), ())),
                                    preferred_element_type=F32)
                s = jnp.where(valid, s, -jnp.inf)
                sink = sink_ref[head]
                m = jnp.maximum(jnp.max(s, axis=-1, keepdims=True), sink)
                p = jnp.exp(s - m)
                denom = jnp.sum(p, axis=-1, keepdims=True) + jnp.exp(sink - m)
                outs.append(_dot(p.astype(BF16), vb) / denom)
            o_ref[:, tile * LANES:(tile + 1) * LANES] = (
                jnp.where(low, outs[0], outs[1]).astype(BF16))


def _mixers_kernel(sink_ref, q_ref, k_ref, v_ref, cum_ref, swq_ref, kp_ref, kc_ref, vp_ref,
                   vc_ref, o_ref, osw_ref, carry_ref, acc_ref, live_ref):
    t = SB_BLOCK
    units = q_ref.shape[1] // LANES
    i = pl.program_id(1)
    lane = lax.broadcasted_iota(jnp.int32, (t, LANES), 1)
    low = lane < HEAD_DIM

    row = lax.broadcasted_iota(jnp.int32, (2 * t, t), 0) % t
    col = lax.broadcasted_iota(jnp.int32, (2 * t, t), 1)
    strict = col < row

    def tile(u):
        return slice(u * LANES, (u + 1) * LANES)

    def stacked_q(u):
        q = q_ref[:, tile(u)]
        zero = jnp.zeros_like(q)
        return jnp.concatenate([jnp.where(low, q, zero), jnp.where(low, zero, q)], axis=0)

    def neg_scores(u, start, blocks):
        kb = k_ref[pl.ds(start, blocks * t), tile(u)]
        return lax.dot_general(stacked_q(u), kb, (((1,), (1,)), ((), ())),
                               preferred_element_type=F32)

    def block_weights(nz, carry, diagonal):
        if diagonal:
            nz = jnp.where(strict, nz, SB_MASKED)
        log_keep = jnp.minimum(nz, 0.0) - jnp.log(1.0 + jnp.exp2(jnp.abs(nz) * -LOG2_E))
        cs = _dot(log_keep.astype(BF16), cum_ref[...])
        log_w = cs[:, :t] - nz
        if carry is not None:
            log_w = log_w + carry
        return jnp.exp(log_w).astype(BF16), cs[:, t:]

    def set_live(carries):
        tiles = [_max_rows_to_tile(c) for c in carries]
        while len(tiles) > 1:
            tiles = [jnp.maximum(a, b) for a, b in zip(tiles[::2], tiles[1::2])]
        live_ref[0] = jnp.max(tiles[0])

    def key_block(j, diagonal):
        start = pl.multiple_of(j * t, t)
        carries = []
        for u in range(units):
            nz = neg_scores(u, start, 1)
            w, tot = block_weights(nz, None if diagonal else carry_ref[u], diagonal)
            pv = _dot(w, v_ref[pl.ds(start, t), tile(u)])
            if diagonal:
                acc_ref[u] = pv
            else:
                acc_ref[u] += pv
                tot = carry_ref[u] + tot
            carry_ref[u] = tot
            carries.append(tot)
        set_live(carries)

    def sliding_window():
        _swa_block(i, sink_ref, swq_ref, kp_ref, kc_ref, vp_ref, vc_ref, osw_ref)

    @pl.when(i >= SB_BAND - 1)
    def _band():
        start = pl.multiple_of((i - (SB_BAND - 1)) * t, t)
        carries = []
        for u in range(units):
            nz = neg_scores(u, start, SB_BAND)
            carry = None
            ws = [None] * SB_BAND
            for blk in reversed(range(SB_BAND)):
                ws[blk], tot = block_weights(nz[:, blk * t:(blk + 1) * t], carry,
                                             blk == SB_BAND - 1)
                carry = tot if carry is None else carry + tot
            acc_ref[u] = _dot(jnp.concatenate(ws, axis=1),
                              v_ref[pl.ds(start, SB_BAND * t), tile(u)])
            carry_ref[u] = carry
            carries.append(carry)
        set_live(carries)
        sliding_window()

    @pl.when(i < SB_BAND - 1)
    def _edge():
        key_block(i, True)
        sliding_window()

    def cond(j):
        return (j >= 0) & (live_ref[0] > SB_SKIP_LOG)

    def body(j):
        key_block(j, False)
        return j - 1

    lax.while_loop(cond, body, jnp.where(i >= SB_BAND - 1, i - SB_BAND, i - 1))

    for u in range(units):
        acc = acc_ref[u]
        o_ref[:, tile(u)] = jnp.where(low, acc[:t], acc[t:]).astype(BF16)


def _cumsum_matrix(t):
    r = np.arange(t)[:, None]
    c = np.arange(2 * t)[None, :]
    return jnp.asarray((c >= t) | (r >= c), dtype=BF16)


def _token_mixers(qkv3, sinks):
    b, s, _ = qkv3.shape
    t = SB_BLOCK
    assert WINDOW == t
    kq, kk, kv = COL_SB_Q // SB_WIDTH, COL_SB_K // SB_WIDTH, COL_SB_V // SB_WIDTH
    kv_w = 2 * SWA_KV_WIDTH
    cq, ck, cv = COL_SW_Q // SWA_Q_WIDTH, COL_SW_K // kv_w, COL_SW_V // kv_w
    prev = lambda i: jnp.maximum(i - 1, 0)
    out = jax.ShapeDtypeStruct((b, s, SB_WIDTH), BF16)
    out_spec = pl.BlockSpec((None, t, SB_WIDTH), lambda bi, i: (bi, i, 0))
    return pl.pallas_call(
        _mixers_kernel,
        out_shape=(out, out),
        grid=(b, s // t),
        in_specs=[
            pl.BlockSpec(memory_space=pltpu.SMEM),
            pl.BlockSpec((None, t, SB_WIDTH), lambda bi, i: (bi, i, kq)),
            pl.BlockSpec((None, s, SB_WIDTH), lambda bi, i: (bi, 0, kk)),
            pl.BlockSpec((None, s, SB_WIDTH), lambda bi, i: (bi, 0, kv)),
            pl.BlockSpec((t, 2 * t), lambda bi, i: (0, 0)),
            pl.BlockSpec((None, t, SWA_Q_WIDTH), lambda bi, i: (bi, i, cq)),
            pl.BlockSpec((None, t, kv_w), lambda bi, i: (bi, prev(i), ck)),
            pl.BlockSpec((None, t, kv_w), lambda bi, i: (bi, i, ck)),
            pl.BlockSpec((None, t, kv_w), lambda bi, i: (bi, prev(i), cv)),
            pl.BlockSpec((None, t, kv_w), lambda bi, i: (bi, i, cv)),
        ],
        out_specs=(out_spec, out_spec),
        scratch_shapes=[pltpu.VMEM((SB_UNITS, 2 * t, t), F32),
                        pltpu.VMEM((SB_UNITS, 2 * t, LANES), F32),
                        pltpu.SMEM((1,), F32)],
        compiler_params=pltpu.CompilerParams(
            dimension_semantics=("arbitrary", "arbitrary"), vmem_limit_bytes=VMEM_LIMIT),
        name="token_mixers",
    )(sinks, qkv3, qkv3, qkv3, _cumsum_matrix(t), qkv3, qkv3, qkv3, qkv3, qkv3)


def _merge_kernel(x_ref, osb_ref, osw_ref, g_ref, wg_ref, wsb_ref, wsw_ref, wo_ref, out_ref):
    x = x_ref[...]
    h = _rms_norm_rows(x, g_ref[...]).astype(BF16)
    osb = osb_ref[...]
    osw = osw_ref[...]
    chunk = 512
    merged = []
    for c0 in range(0, D_MODEL, chunk):
        y_sb = _dot(osb, wsb_ref[:, c0:c0 + chunk])
        y_sw = _dot(osw, wsw_ref[:, c0:c0 + chunk])
        g_sb = jax.nn.sigmoid(_dot(h, wg_ref[:, c0:c0 + chunk]))
        g_sw = jax.nn.sigmoid(_dot(h, wg_ref[:, D_MODEL + c0:D_MODEL + c0 + chunk]))
        merged.append((g_sb * y_sb + g_sw * y_sw).astype(BF16))
    merged = jnp.concatenate(merged, axis=1)
    for c0 in range(0, D_MODEL, chunk):
        out_ref[:, c0:c0 + chunk] = x[:, c0:c0 + chunk] + _dot(merged, wo_ref[:, c0:c0 + chunk])


def _merge_out(x2, o_sb, o_sw, g, w_gate, w_bsb, w_bsw, w_out):
    n = x2.shape[0]
    tm = min(TOKEN_TILE, n)
    const = lambda shape: pl.BlockSpec(shape, lambda t: (0, 0))
    row = lambda width: pl.BlockSpec((tm, width), lambda t: (t, 0))
    return pl.pallas_call(
        _merge_kernel,
        out_shape=jax.ShapeDtypeStruct((n, D_MODEL), F32),
        grid=(n // tm,),
        in_specs=[
            row(D_MODEL), row(SB_WIDTH), row(SWA_Q_WIDTH),
            const((1, D_MODEL)),
            const((D_MODEL, 2 * D_MODEL)),
            const((SB_WIDTH, D_MODEL)),
            const((SWA_Q_WIDTH, D_MODEL)),
            const((D_MODEL, D_MODEL)),
        ],
        out_specs=row(D_MODEL),
        compiler_params=pltpu.CompilerParams(
            dimension_semantics=("arbitrary",), vmem_limit_bytes=VMEM_LIMIT),
        name="merge_outproj",
    )(x2, o_sb, o_sw, g, w_gate, w_bsb, w_bsw, w_out)


def _mlp_kernel(x_ref, g_ref, wu_ref, wd_ref, out_ref):
    x = x_ref[...]
    h = _rms_norm_rows(x, g_ref[...]).astype(BF16)
    acc = x
    for c0 in range(0, D_FF, FF_CHUNK):
        u = jnp.maximum(_dot(h, wu_ref[:, c0:c0 + FF_CHUNK]), 0.0)
        acc = acc + _dot((u * u).astype(BF16), wd_ref[c0:c0 + FF_CHUNK, :])
    out_ref[...] = acc


def _mlp(x2, g, w_up, w_down):
    n = x2.shape[0]
    tm = min(TOKEN_TILE, n)
    resident = lambda shape: pl.BlockSpec(shape, lambda t: (0, 0), pipeline_mode=pl.Buffered(1))
    row = pl.BlockSpec((tm, D_MODEL), lambda t: (t, 0))
    return pl.pallas_call(
        _mlp_kernel,
        out_shape=jax.ShapeDtypeStruct((n, D_MODEL), F32),
        grid=(n // tm,),
        in_specs=[row, resident((1, D_MODEL)), resident((D_MODEL, D_FF)),
                  resident((D_FF, D_MODEL))],
        out_specs=row,
        compiler_params=pltpu.CompilerParams(
            dimension_semantics=("arbitrary",), vmem_limit_bytes=VMEM_LIMIT),
        name="relu2_mlp",
    )(x2, g, w_up, w_down)


def _rope_tables(seq):
    inv_freq = 1.0 / (ROPE_THETA ** (jnp.arange(0, HEAD_DIM, 2, dtype=F32) / HEAD_DIM))
    ang = jnp.arange(seq, dtype=F32)[:, None] * inv_freq[None, :]
    cos, sin = jnp.cos(ang), jnp.sin(ang)
    reps = LANES // HEAD_DIM
    cos_t = jnp.tile(jnp.concatenate([cos, cos], axis=1), (1, reps))
    sin_t = jnp.tile(jnp.concatenate([-sin, sin], axis=1), (1, reps))
    return cos_t, sin_t


def _head_mean_matrix():
    idx = np.arange(LANES) // HEAD_DIM
    m = (idx[:, None] == idx[None, :]).astype(np.float32) / HEAD_DIM
    return jnp.asarray(np.concatenate([m, m], axis=0), dtype=BF16)


@jax.jit
def _forward(x, mix_norm_g, w_in, q_norm_g, k_norm_g, sinks, w_branch_sb, w_branch_swa,
             w_out, mlp_norm_g, w_up, w_down):
    b, s, d = x.shape
    depth = w_in.shape[0]
    cos_t, sin_t = _rope_tables(s)
    head_mean = _head_mean_matrix()
    x2 = x.reshape(b * s, d)
    for l in range(depth):
        w_qkv = w_in[l][:, :W_QKV_WIDTH].astype(BF16)
        w_gate = w_in[l][:, W_QKV_WIDTH:].astype(BF16)
        qg = jnp.tile(q_norm_g[l], HEADS_PER_TILE)[None, :]
        kg = jnp.tile(k_norm_g[l], HEADS_PER_TILE)[None, :]

        qkv = _inproj(x2, mix_norm_g[l][None, :], w_qkv, qg, kg, cos_t, sin_t, head_mean, s)
        o_sb, o_sw = _token_mixers(qkv.reshape(b, s, QKV_WIDTH), sinks[l])
        x2 = _merge_out(x2, o_sb.reshape(b * s, SB_WIDTH), o_sw.reshape(b * s, SWA_Q_WIDTH),
                        mix_norm_g[l][None, :], w_gate,
                        w_branch_sb[l].astype(BF16), w_branch_swa[l].astype(BF16),
                        w_out[l].astype(BF16))
        x2 = _mlp(x2, mlp_norm_g[l][None, :], w_up[l].astype(BF16), w_down[l].astype(BF16))
    return x2.reshape(b, s, d)


def kernel(x, mix_norm_g, w_in, q_norm_g, k_norm_g, sinks, w_branch_sb, w_branch_swa, w_out,
           mlp_norm_g, w_up, w_down):
    return _forward(x, mix_norm_g, w_in, q_norm_g, k_norm_g, sinks, w_branch_sb, w_branch_swa,
                    w_out, mlp_norm_g, w_up, w_down)
```

```python
import jax
import jax.numpy as jnp
import numpy as np
from jax import lax
from jax.experimental import pallas as pl
from jax.experimental.pallas import tpu as pltpu

D_MODEL = 1024
HEAD_DIM = 64
SB_HEADS = 8
SWA_HEADS = 8
SWA_KV_HEADS = 2
WINDOW = 128
D_FF = 4 * D_MODEL
ROPE_THETA = 10000.0
NORM_EPS = 1e-6
SB_WIDTH = SB_HEADS * HEAD_DIM
SWA_Q_WIDTH = SWA_HEADS * HEAD_DIM
SWA_KV_WIDTH = SWA_KV_HEADS * HEAD_DIM
SCALE = HEAD_DIM ** -0.5
LOG2_E = 1.4426950408889634

LANES = 128
HEADS_PER_TILE = LANES // HEAD_DIM

COL_SB_Q = 0
COL_SB_K = COL_SB_Q + SB_WIDTH
COL_SB_V = COL_SB_K + SB_WIDTH
COL_SW_Q = COL_SB_V + SB_WIDTH
COL_SW_K = COL_SW_Q + SWA_Q_WIDTH
COL_SW_V = COL_SW_K + 2 * SWA_KV_WIDTH
QKV_WIDTH = COL_SW_V + 2 * SWA_KV_WIDTH
W_COL_KV = COL_SW_K
W_QKV_WIDTH = W_COL_KV + 2 * SWA_KV_WIDTH

TOKEN_TILE = 1024
SB_BLOCK = 128
SB_BAND = 4
SB_UNITS = SB_WIDTH // LANES
SB_SKIP_LOG = -120.0
SB_MASKED = 1e30
FF_CHUNK = 512
VMEM_LIMIT = 56 * 1024 * 1024

F32 = jnp.float32
BF16 = jnp.bfloat16


def _dot(a, b):
    return jnp.dot(a, b, preferred_element_type=F32)


def _rms_norm_rows(x, g):
    ms = jnp.mean(x * x, axis=-1, keepdims=True)
    return x * lax.rsqrt(ms + NORM_EPS) * g


def _split_bf16(x):
    hi = x.astype(BF16)
    lo = (x - hi.astype(F32)).astype(BF16)
    return hi, lo


def _inproj_kernel(x_ref, g_ref, w_ref, qg_ref, kg_ref, cos_ref, sin_ref, hm_ref, out_ref):
    tm = x_ref.shape[0]
    h = _rms_norm_rows(x_ref[...], g_ref[...]).astype(BF16)

    def proj(c0, width):
        return _dot(h, w_ref[:, c0:c0 + width])

    cos = cos_ref[...]
    sin = sin_ref[...]
    lane = lax.broadcasted_iota(jnp.int32, (tm, LANES), 1)
    first_half = (lane % HEAD_DIM) < (HEAD_DIM // 2)

    def norm_rope(y, gain):
        hi, lo = _split_bf16(y * y)
        ms = _dot(jnp.concatenate([hi, lo], axis=1), hm_ref[...])
        yn = y * lax.rsqrt(ms + NORM_EPS) * gain
        rot = jnp.where(first_half,
                        pltpu.roll(yn, LANES - HEAD_DIM // 2, 1),
                        pltpu.roll(yn, HEAD_DIM // 2, 1))
        return yn * cos + rot * sin

    q = proj(COL_SW_Q, SWA_Q_WIDTH)
    for c in range(SWA_Q_WIDTH // LANES):
        y = norm_rope(q[:, c * LANES:(c + 1) * LANES], qg_ref[...]) * SCALE
        out_ref[:, COL_SW_Q + c * LANES:COL_SW_Q + (c + 1) * LANES] = y.astype(BF16)
    low = lane < HEAD_DIM
    kv = proj(W_COL_KV, 2 * SWA_KV_WIDTH)
    for col, y in ((COL_SW_K, norm_rope(kv[:, :LANES], kg_ref[...])), (COL_SW_V, kv[:, LANES:])):
        swapped = pltpu.roll(y, HEAD_DIM, 1)
        out_ref[:, col:col + LANES] = jnp.where(low, y, swapped).astype(BF16)
        out_ref[:, col + LANES:col + 2 * LANES] = jnp.where(low, swapped, y).astype(BF16)

    out_ref[:, COL_SB_Q:COL_SB_K] = (proj(COL_SB_Q, SB_WIDTH) * -SCALE).astype(BF16)
    out_ref[:, COL_SB_K:COL_SB_V] = proj(COL_SB_K, SB_WIDTH).astype(BF16)
    out_ref[:, COL_SB_V:COL_SW_Q] = proj(COL_SB_V, SB_WIDTH).astype(BF16)


def _inproj(x2, g, w_qkv, qg, kg, cos_t, sin_t, head_mean, seq):
    n = x2.shape[0]
    tm = min(TOKEN_TILE, seq)
    pos_blocks = seq // tm
    const = lambda shape: pl.BlockSpec(shape, lambda t: (0, 0))
    return pl.pallas_call(
        _inproj_kernel,
        out_shape=jax.ShapeDtypeStruct((n, QKV_WIDTH), BF16),
        grid=(n // tm,),
        in_specs=[
            pl.BlockSpec((tm, D_MODEL), lambda t: (t, 0)),
            const((1, D_MODEL)),
            const((D_MODEL, W_QKV_WIDTH)),
            const((1, LANES)),
            const((1, LANES)),
            pl.BlockSpec((tm, LANES), lambda t: (t % pos_blocks, 0)),
            pl.BlockSpec((tm, LANES), lambda t: (t % pos_blocks, 0)),
            const((2 * LANES, LANES)),
        ],
        out_specs=pl.BlockSpec((tm, QKV_WIDTH), lambda t: (t, 0)),
        compiler_params=pltpu.CompilerParams(
            dimension_semantics=("arbitrary",), vmem_limit_bytes=VMEM_LIMIT),
        name="norm_inproj",
    )(x2, g, w_qkv, qg, kg, cos_t, sin_t, head_mean)


def _max_rows_to_tile(x):
    while x.shape[0] > 8:
        half = x.shape[0] // 2
        x = jnp.maximum(x[:half], x[half:])
    return x


def _swa_block(n, sink_ref, q_ref, kp_ref, kc_ref, vp_ref, vc_ref, o_ref):
    t = WINDOW
    group = SWA_HEADS // SWA_KV_HEADS
    lane = lax.broadcasted_iota(jnp.int32, (t, LANES), 1)
    low = lane < HEAD_DIM

    i_idx = lax.broadcasted_iota(jnp.int32, (t, 2 * t), 0)
    j_idx = lax.broadcasted_iota(jnp.int32, (t, 2 * t), 1)
    rel = j_idx - t - i_idx
    valid = (rel <= 0) & (rel > -WINDOW) & (j_idx >= t - n * t)

    for g in range(SWA_KV_HEADS):
        kb = jnp.concatenate([kp_ref[:, g * LANES:(g + 1) * LANES],
                              kc_ref[:, g * LANES:(g + 1) * LANES]], axis=0)
        vb = jnp.concatenate([vp_ref[:, g * LANES:(g + 1) * LANES],
                              vc_ref[:, g * LANES:(g + 1) * LANES]], axis=0)
        for c in range(group // HEADS_PER_TILE):
            tile = g * (group // HEADS_PER_TILE) + c
            q = q_ref[:, tile * LANES:(tile + 1) * LANES]
            zero = jnp.zeros_like(q)
            outs = []
            for half in range(HEADS_PER_TILE):
                head = tile * HEADS_PER_TILE + half
                qh = jnp.where(low if half == 0 else ~low, q, zero)
                s = lax.dot_general(qh, kb, (((1,), (1,)), ((), ())),
                                    preferred_element_type=F32)
                s = jnp.where(valid, s, -jnp.inf)
                sink = sink_ref[head]
                m = jnp.maximum(jnp.max(s, axis=-1, keepdims=True), sink)
                p = jnp.exp(s - m)
                denom = jnp.sum(p, axis=-1, keepdims=True) + jnp.exp(sink - m)
                outs.append(_dot(p.astype(BF16), vb) / denom)
            o_ref[:, tile * LANES:(tile + 1) * LANES] = (
                jnp.where(low, outs[0], outs[1]).astype(BF16))


def _mixers_kernel(sink_ref, q_ref, k_ref, v_ref, cum_ref, swq_ref, kp_ref, kc_ref, vp_ref,
                   vc_ref, o_ref, osw_ref, carry_ref, acc_ref, live_ref):
    t = SB_BLOCK
    units = q_ref.shape[1] // LANES
    i = pl.program_id(1)
    lane = lax.broadcasted_iota(jnp.int32, (t, LANES), 1)
    low = lane < HEAD_DIM

    row = lax.broadcasted_iota(jnp.int32, (2 * t, t), 0) % t
    col = lax.broadcasted_iota(jnp.int32, (2 * t, t), 1)
    strict = col < row

    def tile(u):
        return slice(u * LANES, (u + 1) * LANES)

    def stacked_q(u):
        q = q_ref[:, tile(u)]
        zero = jnp.zeros_like(q)
        return jnp.concatenate([jnp.where(low, q, zero), jnp.where(low, zero, q)], axis=0)

    def neg_scores(u, start, blocks):
        kb = k_ref[pl.ds(start, blocks * t), tile(u)]
        return lax.dot_general(stacked_q(u), kb, (((1,), (1,)), ((), ())),
                               preferred_element_type=F32)

    def block_weights(nz, carry, diagonal):
        if diagonal:
            nz = jnp.where(strict, nz, SB_MASKED)
        log_keep = jnp.minimum(nz, 0.0) - jnp.log(1.0 + jnp.exp2(jnp.abs(nz) * -LOG2_E))
        cs = _dot(log_keep.astype(BF16), cum_ref[...])
        log_w = cs[:, :t] - nz
        if carry is not None:
            log_w = log_w + carry
        return jnp.exp(log_w).astype(BF16), cs[:, t:]

    def set_live(carries):
        tiles = [_max_rows_to_tile(c) for c in carries]
        while len(tiles) > 1:
            tiles = [jnp.maximum(a, b) for a, b in zip(tiles[::2], tiles[1::2])]
        live_ref[0] = jnp.max(tiles[0])

    def key_block(j, diagonal):
        start = pl.multiple_of(j * t, t)
        carries = []
        for u in range(units):
            nz = neg_scores(u, start, 1)
            w, tot = block_weights(nz, None if diagonal else carry_ref[u], diagonal)
            pv = _dot(w, v_ref[pl.ds(start, t), tile(u)])
            if diagonal:
                acc_ref[u] = pv
            else:
                acc_ref[u] += pv
                tot = carry_ref[u] + tot
            carry_ref[u] = tot
            carries.append(tot)
        set_live(carries)

    def sliding_window():
        _swa_block(i, sink_ref, swq_ref, kp_ref, kc_ref, vp_ref, vc_ref, osw_ref)

    @pl.when(i >= SB_BAND - 1)
    def _band():
        start = pl.multiple_of((i - (SB_BAND - 1)) * t, t)
        nz = [neg_scores(u, start, SB_BAND) for u in range(units)]
        carries = [None] * units
        ws = [[None] * SB_BAND for _ in range(units)]
        for blk in reversed(range(SB_BAND)):
            for u in range(units):
                ws[u][blk], tot = block_weights(nz[u][:, blk * t:(blk + 1) * t], carries[u],
                                                blk == SB_BAND - 1)
                carries[u] = tot if carries[u] is None else carries[u] + tot
        for u in range(units):
            acc_ref[u] = _dot(jnp.concatenate(ws[u], axis=1),
                              v_ref[pl.ds(start, SB_BAND * t), tile(u)])
            carry_ref[u] = carries[u]
        set_live(carries)
        sliding_window()

    @pl.when(i < SB_BAND - 1)
    def _edge():
        key_block(i, True)
        sliding_window()

    def cond(j):
        return (j >= 0) & (live_ref[0] > SB_SKIP_LOG)

    def body(j):
        key_block(j, False)
        return j - 1

    lax.while_loop(cond, body, jnp.where(i >= SB_BAND - 1, i - SB_BAND, i - 1))

    for u in range(units):
        acc = acc_ref[u]
        o_ref[:, tile(u)] = jnp.where(low, acc[:t], acc[t:]).astype(BF16)


def _cumsum_matrix(t):
    r = np.arange(t)[:, None]
    c = np.arange(2 * t)[None, :]
    return jnp.asarray((c >= t) | (r >= c), dtype=BF16)


def _token_mixers(qkv3, sinks):
    b, s, _ = qkv3.shape
    t = SB_BLOCK
    assert WINDOW == t
    kq, kk, kv = COL_SB_Q // SB_WIDTH, COL_SB_K // SB_WIDTH, COL_SB_V // SB_WIDTH
    kv_w = 2 * SWA_KV_WIDTH
    cq, ck, cv = COL_SW_Q // SWA_Q_WIDTH, COL_SW_K // kv_w, COL_SW_V // kv_w
    prev = lambda i: jnp.maximum(i - 1, 0)
    out = jax.ShapeDtypeStruct((b, s, SB_WIDTH), BF16)
    out_spec = pl.BlockSpec((None, t, SB_WIDTH), lambda bi, i: (bi, i, 0))
    return pl.pallas_call(
        _mixers_kernel,
        out_shape=(out, out),
        grid=(b, s // t),
        in_specs=[
            pl.BlockSpec(memory_space=pltpu.SMEM),
            pl.BlockSpec((None, t, SB_WIDTH), lambda bi, i: (bi, i, kq)),
            pl.BlockSpec((None, s, SB_WIDTH), lambda bi, i: (bi, 0, kk)),
            pl.BlockSpec((None, s, SB_WIDTH), lambda bi, i: (bi, 0, kv)),
            pl.BlockSpec((t, 2 * t), lambda bi, i: (0, 0)),
            pl.BlockSpec((None, t, SWA_Q_WIDTH), lambda bi, i: (bi, i, cq)),
            pl.BlockSpec((None, t, kv_w), lambda bi, i: (bi, prev(i), ck)),
            pl.BlockSpec((None, t, kv_w), lambda bi, i: (bi, i, ck)),
            pl.BlockSpec((None, t, kv_w), lambda bi, i: (bi, prev(i), cv)),
            pl.BlockSpec((None, t, kv_w), lambda bi, i: (bi, i, cv)),
        ],
        out_specs=(out_spec, out_spec),
        scratch_shapes=[pltpu.VMEM((SB_UNITS, 2 * t, t), F32),
                        pltpu.VMEM((SB_UNITS, 2 * t, LANES), F32),
                        pltpu.SMEM((1,), F32)],
        compiler_params=pltpu.CompilerParams(
            dimension_semantics=("arbitrary", "arbitrary"), vmem_limit_bytes=VMEM_LIMIT),
        name="token_mixers",
    )(sinks, qkv3, qkv3, qkv3, _cumsum_matrix(t), qkv3, qkv3, qkv3, qkv3, qkv3)


def _merge_kernel(x_ref, osb_ref, osw_ref, g_ref, wg_ref, wsb_ref, wsw_ref, wo_ref, out_ref):
    x = x_ref[...]
    h = _rms_norm_rows(x, g_ref[...]).astype(BF16)
    osb = osb_ref[...]
    osw = osw_ref[...]
    chunk = 512
    merged = []
    for c0 in range(0, D_MODEL, chunk):
        y_sb = _dot(osb, wsb_ref[:, c0:c0 + chunk])
        y_sw = _dot(osw, wsw_ref[:, c0:c0 + chunk])
        g_sb = jax.nn.sigmoid(_dot(h, wg_ref[:, c0:c0 + chunk]))
        g_sw = jax.nn.sigmoid(_dot(h, wg_ref[:, D_MODEL + c0:D_MODEL + c0 + chunk]))
        merged.append((g_sb * y_sb + g_sw * y_sw).astype(BF16))
    merged = jnp.concatenate(merged, axis=1)
    for c0 in range(0, D_MODEL, chunk):
        out_ref[:, c0:c0 + chunk] = x[:, c0:c0 + chunk] + _dot(merged, wo_ref[:, c0:c0 + chunk])


def _merge_out(x2, o_sb, o_sw, g, w_gate, w_bsb, w_bsw, w_out):
    n = x2.shape[0]
    tm = min(TOKEN_TILE, n)
    const = lambda shape: pl.BlockSpec(shape, lambda t: (0, 0))
    row = lambda width: pl.BlockSpec((tm, width), lambda t: (t, 0))
    return pl.pallas_call(
        _merge_kernel,
        out_shape=jax.ShapeDtypeStruct((n, D_MODEL), F32),
        grid=(n // tm,),
        in_specs=[
            row(D_MODEL), row(SB_WIDTH), row(SWA_Q_WIDTH),
            const((1, D_MODEL)),
            const((D_MODEL, 2 * D_MODEL)),
            const((SB_WIDTH, D_MODEL)),
            const((SWA_Q_WIDTH, D_MODEL)),
            const((D_MODEL, D_MODEL)),
        ],
        out_specs=row(D_MODEL),
        compiler_params=pltpu.CompilerParams(
            dimension_semantics=("arbitrary",), vmem_limit_bytes=VMEM_LIMIT),
        name="merge_outproj",
    )(x2, o_sb, o_sw, g, w_gate, w_bsb, w_bsw, w_out)


def _mlp_kernel(x_ref, g_ref, wu_ref, wd_ref, out_ref):
    x = x_ref[...]
    h = _rms_norm_rows(x, g_ref[...]).astype(BF16)
    acc = x
    for c0 in range(0, D_FF, FF_CHUNK):
        u = jnp.maximum(_dot(h, wu_ref[:, c0:c0 + FF_CHUNK]), 0.0)
        acc = acc + _dot((u * u).astype(BF16), wd_ref[c0:c0 + FF_CHUNK, :])
    out_ref[...] = acc


def _mlp(x2, g, w_up, w_down):
    n = x2.shape[0]
    tm = min(TOKEN_TILE, n)
    resident = lambda shape: pl.BlockSpec(shape, lambda t: (0, 0), pipeline_mode=pl.Buffered(1))
    row = pl.BlockSpec((tm, D_MODEL), lambda t: (t, 0))
    return pl.pallas_call(
        _mlp_kernel,
        out_shape=jax.ShapeDtypeStruct((n, D_MODEL), F32),
        grid=(n // tm,),
        in_specs=[row, resident((1, D_MODEL)), resident((D_MODEL, D_FF)),
                  resident((D_FF, D_MODEL))],
        out_specs=row,
        compiler_params=pltpu.CompilerParams(
            dimension_semantics=("arbitrary",), vmem_limit_bytes=VMEM_LIMIT),
        name="relu2_mlp",
    )(x2, g, w_up, w_down)


def _rope_tables(seq):
    inv_freq = 1.0 / (ROPE_THETA ** (jnp.arange(0, HEAD_DIM, 2, dtype=F32) / HEAD_DIM))
    ang = jnp.arange(seq, dtype=F32)[:, None] * inv_freq[None, :]
    cos, sin = jnp.cos(ang), jnp.sin(ang)
    reps = LANES // HEAD_DIM
    cos_t = jnp.tile(jnp.concatenate([cos, cos], axis=1), (1, reps))
    sin_t = jnp.tile(jnp.concatenate([-sin, sin], axis=1), (1, reps))
    return cos_t, sin_t


def _head_mean_matrix():
    idx = np.arange(LANES) // HEAD_DIM
    m = (idx[:, None] == idx[None, :]).astype(np.float32) / HEAD_DIM
    return jnp.asarray(np.concatenate([m, m], axis=0), dtype=BF16)


@jax.jit
def _forward(x, mix_norm_g, w_in, q_norm_g, k_norm_g, sinks, w_branch_sb, w_branch_swa,
             w_out, mlp_norm_g, w_up, w_down):
    b, s, d = x.shape
    depth = w_in.shape[0]
    cos_t, sin_t = _rope_tables(s)
    head_mean = _head_mean_matrix()
    x2 = x.reshape(b * s, d)
    for l in range(depth):
        w_qkv = w_in[l][:, :W_QKV_WIDTH].astype(BF16)
        w_gate = w_in[l][:, W_QKV_WIDTH:].astype(BF16)
        qg = jnp.tile(q_norm_g[l], HEADS_PER_TILE)[None, :]
        kg = jnp.tile(k_norm_g[l], HEADS_PER_TILE)[None, :]

        qkv = _inproj(x2, mix_norm_g[l][None, :], w_qkv, qg, kg, cos_t, sin_t, head_mean, s)
        o_sb, o_sw = _token_mixers(qkv.reshape(b, s, QKV_WIDTH), sinks[l])
        x2 = _merge_out(x2, o_sb.reshape(b * s, SB_WIDTH), o_sw.reshape(b * s, SWA_Q_WIDTH),
                        mix_norm_g[l][None, :], w_gate,
                        w_branch_sb[l].astype(BF16), w_branch_swa[l].astype(BF16),
                        w_out[l].astype(BF16))
        x2 = _mlp(x2, mlp_norm_g[l][None, :], w_up[l].astype(BF16), w_down[l].astype(BF16))
    return x2.reshape(b, s, d)


def kernel(x, mix_norm_g, w_in, q_norm_g, k_norm_g, sinks, w_branch_sb, w_branch_swa, w_out,
           mlp_norm_g, w_up, w_down):
    return _forward(x, mix_norm_g, w_in, q_norm_g, k_norm_g, sinks, w_branch_sb, w_branch_swa,
                    w_out, mlp_norm_g, w_up, w_down)
```
